```python
import math
import functools
import jax, jax.numpy as jnp
from jax import lax
import numpy as np

D_MODEL = 2048
BATCH = 2
SEQ = 4096
DEPTH = 1
DEC_BATCH = 32
DEC_SEQ = 4
PAST_LEN = 16384
PAGE_SIZE = 128

A_HEADS = 8
A_HEAD_DIM = 64
A_WIDTH = A_HEADS * 2 * A_HEAD_DIM
Q_BLOCK = 128
G_HEADS = 8
G_DK = 128
G_DV = 128
G_KW = G_HEADS * G_DK
G_VW = G_HEADS * G_DV
CONV_W = 4
CONV_CH = 2 * G_KW + G_VW
CHUNK = 64
D_FF = -(-8 * D_MODEL // (3 * 256)) * 256
IN_SIZES = (A_WIDTH, A_WIDTH, A_WIDTH, G_KW, G_KW, G_VW, G_VW, G_HEADS, G_HEADS, 2 * D_MODEL)
D_IN = sum(IN_SIZES)
EPS = 1e-6
F32 = jnp.float32

kernel_name = 'hybrid_diffattn_gdeltanet_step'


def rms_norm(x, g):
    x32 = x.astype(F32)
    y = x32 * lax.rsqrt(jnp.mean(x32 * x32, axis=-1, keepdims=True) + EPS)
    return (y * g.astype(F32)).astype(x.dtype)


def l2_normalize(x):
    return x * lax.rsqrt(jnp.sum(x * x, axis=-1, keepdims=True) + EPS)


def alibi_slopes():
    return 2.0 ** (-8.0 * jnp.arange(1, A_HEADS + 1, dtype=F32) / A_HEADS)


def adaln(c, w_ada, b_ada):
    mod = jax.nn.silu(c) @ w_ada + b_ada
    return jnp.split(mod[:, None, :], 6, axis=-1)


def modulate(x, g, shift, scale):
    return rms_norm(x, g) * (1 + scale) + shift


def split_in(z):
    points = [int(p) for p in np.cumsum(IN_SIZES)[:-1]]
    return jnp.split(z, points, axis=-1)


def diff_attn_prompt(q, k, v, lam, sub_g, lam_init):
    B, L = q.shape[:2]
    slopes = alibi_slopes()
    kpos = jnp.arange(L)
    scale = A_HEAD_DIM ** -0.5

    def block(i):
        start = i * Q_BLOCK
        qb = lax.dynamic_slice_in_dim(q, start, Q_BLOCK, axis=1)
        qpos = start + jnp.arange(Q_BLOCK)
        dist = (qpos[:, None] - kpos[None, :]).astype(F32)
        s = jnp.einsum('bqhmd,bkhmd->bhmqk', qb, k, preferred_element_type=F32) * scale
        s = jnp.where(dist >= 0, s - slopes[:, None, None, None] * dist, -jnp.inf)
        p = jax.nn.softmax(s, axis=-1)
        a = p[:, :, 0] - lam * p[:, :, 1]
        return jnp.einsum('bhqk,bkhe->bqhe', a.astype(v.dtype), v)

    o = lax.map(block, jnp.arange(L // Q_BLOCK))
    o = o.transpose(1, 0, 2, 3, 4).reshape(B, L, A_HEADS, 2 * A_HEAD_DIM)
    o = rms_norm(o, sub_g) * (1.0 - lam_init)
    return o.reshape(B, L, A_WIDTH)


def diff_attn_sample(q, k_new, v_new, cache_k, cache_v, layer, page_table, lam, sub_g, lam_init):
    DB, S = q.shape[:2]
    past = page_table.shape[1] * cache_k.shape[2]
    slopes = alibi_slopes()
    scale = A_HEAD_DIM ** -0.5
    qpos = past + jnp.arange(S)
    kpos = jnp.arange(past + S)
    dist = (qpos[:, None] - kpos[None, :]).astype(F32)
    bias = jnp.where(dist >= 0, -slopes[:, None, None, None] * dist, -jnp.inf)

    def one(args):
        pt, qs, ks, vs = args
        kp = cache_k[layer, pt].reshape(past, A_HEADS, 2, A_HEAD_DIM)
        vp = cache_v[layer, pt].reshape(past, A_HEADS, 2 * A_HEAD_DIM)
        kall = jnp.concatenate([kp.astype(ks.dtype), ks], axis=0)
        vall = jnp.concatenate([vp.astype(vs.dtype), vs], axis=0)
        s = jnp.einsum('qhmd,khmd->hmqk', qs, kall, preferred_element_type=F32) * scale + bias
        p = jax.nn.softmax(s, axis=-1)
        a = p[:, 0] - lam * p[:, 1]
        return jnp.einsum('hqk,khe->qhe', a.astype(vall.dtype), vall)

    o = lax.map(one, (page_table, q, k_new, v_new))
    o = rms_norm(o, sub_g) * (1.0 - lam_init)
    return o.reshape(DB, S, A_WIDTH)


def causal_conv(xc, buf, w):
    L = xc.shape[1]
    xpad = jnp.concatenate([buf, xc], axis=1)
    y = xpad[:, 0:L] * w[0]
    for j in range(1, CONV_W):
        y = y + xpad[:, j:j + L] * w[j]
    return jax.nn.silu(y), xpad[:, L:]


def gated_delta_chunked(q, k, v, g, beta, s0, chunk):
    B, L, H, dk = k.shape
    dv = v.shape[-1]
    n = L // chunk

    def to_chunks(t):
        return t.reshape((B, n, chunk, H) + t.shape[3:]).swapaxes(2, 3)

    qc, kc, vc, gc, bc = (to_chunks(t) for t in (q, k, v, g, beta))
    gcum = jnp.cumsum(gc, axis=-1)
    idx = jnp.arange(chunk)
    causal = idx[:, None] >= idx[None, :]
    strict = idx[:, None] > idx[None, :]
    decay = jnp.exp(jnp.where(causal, gcum[..., :, None] - gcum[..., None, :], -jnp.inf))
    kk = jnp.einsum('bnhid,bnhjd->bnhij', kc, kc)
    a_mat = jnp.where(strict, bc[..., :, None] * kk * decay, 0.0)
    eye = jnp.eye(chunk, dtype=F32)
    t_inv = lax.linalg.triangular_solve(eye + a_mat, jnp.broadcast_to(eye, a_mat.shape),
                                        left_side=True, lower=True, unit_diagonal=True)
    u_base = t_inv @ (vc * bc[..., None])
    w_mat = t_inv @ (kc * (bc * jnp.exp(gcum))[..., None])
    qk = jnp.einsum('bnhid,bnhjd->bnhij', qc, kc) * decay
    q_dec = qc * jnp.exp(gcum)[..., None]
    k_dec = kc * jnp.exp(gcum[..., -1:] - gcum)[..., None]
    g_last = jnp.exp(gcum[..., -1])
    xs = tuple(t.swapaxes(0, 1) for t in (u_base, w_mat, qk, q_dec, k_dec, g_last))

    def step(s, inp):
        ub, wc, qkc, qd, kd, gl = inp
        u = ub - jnp.einsum('bhck,bhkv->bhcv', wc, s)
        o = jnp.einsum('bhck,bhkv->bhcv', qd, s) + jnp.einsum('bhij,bhjv->bhiv', qkc, u)
        s = s * gl[..., None, None] + jnp.einsum('bhck,bhcv->bhkv', kd, u)
        return s, o

    s_fin, o = lax.scan(step, s0, xs)
    return o.transpose(1, 0, 3, 2, 4).reshape(B, L, H, dv), s_fin


def gdn_branch(gq, gk, gv, gz, gb, ga, conv_buf, s0, conv_w, a_log, dt_bias, gdn_norm, chunk):
    B, L, _ = gq.shape
    qkv, conv_new = causal_conv(jnp.concatenate([gq, gk, gv], axis=-1), conv_buf.astype(gq.dtype), conv_w)
    q, k, v = jnp.split(qkv.astype(F32), [G_KW, 2 * G_KW], axis=-1)
    q = l2_normalize(q.reshape(B, L, G_HEADS, G_DK)) * (G_DK ** -0.5)
    k = l2_normalize(k.reshape(B, L, G_HEADS, G_DK))
    v = v.reshape(B, L, G_HEADS, G_DV)
    beta = jax.nn.sigmoid(gb.astype(F32))
    g = -jnp.exp(a_log.astype(F32)) * jax.nn.softplus(ga.astype(F32) + dt_bias.astype(F32))
    o, s_new = gated_delta_chunked(q, k, v, g, beta, s0.astype(F32), chunk)
    o = rms_norm(o, gdn_norm) * jax.nn.silu(gz.astype(F32).reshape(B, L, G_HEADS, G_DV))
    return o.reshape(B, L, G_VW).astype(gq.dtype), conv_new, s_new


def merge_branches(a_out, g_out, gates, w_pa, w_pg, w_o):
    gate = jax.nn.sigmoid(gates.astype(F32)).astype(a_out.dtype)
    gate_a, gate_g = jnp.split(gate, 2, axis=-1)
    return (gate_a * (a_out @ w_pa) + gate_g * (g_out @ w_pg)) @ w_o


def mixer_sublayer(h, attn_fn, conv_buf, s0, chunk, w_in, b_in, conv_w, a_log, dt_bias, gdn_norm, w_pa, w_pg, w_o):
    B, L, _ = h.shape
    aq, ak, av, gq, gk, gv, gz, gb, ga, gates = split_in(h @ w_in + b_in)
    q = aq.reshape(B, L, A_HEADS, 2, A_HEAD_DIM)
    k = ak.reshape(B, L, A_HEADS, 2, A_HEAD_DIM)
    v = av.reshape(B, L, A_HEADS, 2 * A_HEAD_DIM)
    a_out = attn_fn(q, k, v)
    g_out, conv_new, s_new = gdn_branch(gq, gk, gv, gz, gb, ga, conv_buf, s0, conv_w, a_log, dt_bias, gdn_norm, chunk)
    y = merge_branches(a_out, g_out, gates, w_pa, w_pg, w_o)
    return y, k.reshape(B, L, A_HEADS, 2 * A_HEAD_DIM), v, conv_new, s_new


def swiglu(h, w_up, w_down):
    gt, up = jnp.split(h @ w_up, 2, axis=-1)
    return (jax.nn.silu(gt) * up) @ w_down


def setup_inputs(seed: int = 0) -> dict:
    key = jax.random.key(seed)
    ks = jax.random.split(key, 32)
    n_pages = PAST_LEN // PAGE_SIZE
    n_used = DEC_BATCH * n_pages
    n_phys = n_used + max(1, n_used // 4)

    def nrm(k, shape, s):
        return s * jax.random.normal(k, shape, F32)

    def gain(k, shape):
        return 1.0 + 0.05 * jax.random.normal(k, shape, F32)

    page_table = jax.random.permutation(ks[6], n_phys)[:n_used].reshape(DEC_BATCH, n_pages).astype(jnp.int32)
    dt = jnp.exp(jax.random.uniform(ks[17], (DEPTH, G_HEADS), F32, math.log(1e-3), math.log(1e-1)))
    dt_bias = dt + jnp.log(-jnp.expm1(-dt))
    a_log = jnp.log(jax.random.uniform(ks[16], (DEPTH, G_HEADS), F32, 1.0, 16.0))
    return {
        'x_prompt': nrm(ks[0], (BATCH, SEQ, D_MODEL), 1.0),
        'x_sample': nrm(ks[1], (DEC_BATCH, DEC_SEQ, D_MODEL), 1.0),
        'c_prompt': nrm(ks[2], (BATCH, D_MODEL), 1.0),
        'c_sample': nrm(ks[3], (DEC_BATCH, D_MODEL), 1.0),
        'cache_k': nrm(ks[4], (DEPTH, n_phys, PAGE_SIZE, A_HEADS, 2 * A_HEAD_DIM), 1.0),
        'cache_v': nrm(ks[5], (DEPTH, n_phys, PAGE_SIZE, A_HEADS, 2 * A_HEAD_DIM), 1.0),
        'page_table': page_table,
        'state_ssm': nrm(ks[7], (DEPTH, DEC_BATCH, G_HEADS, G_DK, G_DV), 0.1),
        'state_conv': nrm(ks[8], (DEPTH, DEC_BATCH, CONV_W - 1, CONV_CH), 1.0),
        'w_ada': nrm(ks[9], (DEPTH, D_MODEL, 6 * D_MODEL), 0.5 * D_MODEL ** -0.5),
        'b_ada': nrm(ks[10], (DEPTH, 6 * D_MODEL), 0.01),
        'norm_mix_pre': gain(ks[11], (DEPTH, D_MODEL)),
        'norm_mix_post': gain(ks[12], (DEPTH, D_MODEL)),
        'norm_ffn_pre': gain(ks[13], (DEPTH, D_MODEL)),
        'norm_ffn_post': gain(ks[14], (DEPTH, D_MODEL)),
        'w_in': nrm(ks[15], (DEPTH, D_MODEL, D_IN), D_MODEL ** -0.5),
        'b_in': nrm(ks[18], (DEPTH, D_IN), 0.01),
        'conv_w': nrm(ks[19], (DEPTH, CONV_W, CONV_CH), CONV_W ** -0.5),
        'lambda_q1': nrm(ks[20], (DEPTH, A_HEAD_DIM), 0.1),
        'lambda_k1': nrm(ks[21], (DEPTH, A_HEAD_DIM), 0.1),
        'lambda_q2': nrm(ks[22], (DEPTH, A_HEAD_DIM), 0.1),
        'lambda_k2': nrm(ks[23], (DEPTH, A_HEAD_DIM), 0.1),
        'attn_subln': gain(ks[24], (DEPTH, 2 * A_HEAD_DIM)),
        'a_log': a_log,
        'dt_bias': dt_bias,
        'gdn_norm': gain(ks[25], (DEPTH, G_DV)),
        'w_proj_attn': nrm(ks[26], (DEPTH, A_WIDTH, D_MODEL), A_WIDTH ** -0.5),
        'w_proj_gdn': nrm(ks[27], (DEPTH, G_VW, D_MODEL), G_VW ** -0.5),
        'w_out': nrm(ks[28], (DEPTH, D_MODEL, D_MODEL), D_MODEL ** -0.5),
        'w_ffn_up': nrm(ks[29], (DEPTH, D_MODEL, 2 * D_FF), D_MODEL ** -0.5),
        'w_ffn_down': nrm(ks[30], (DEPTH, D_FF, D_MODEL), D_FF ** -0.5),
    }


def reference(x_prompt, x_sample, c_prompt, c_sample, cache_k, cache_v, page_table, state_ssm, state_conv,
              w_ada, b_ada, norm_mix_pre, norm_mix_post, norm_ffn_pre, norm_ffn_post, w_in, b_in, conv_w,
              lambda_q1, lambda_k1, lambda_q2, lambda_k2, attn_subln, a_log, dt_bias, gdn_norm,
              w_proj_attn, w_proj_gdn, w_out, w_ffn_up, w_ffn_down):
    B, L, _ = x_prompt.shape
    DB, S, _ = x_sample.shape
    chunk_p = CHUNK if L % CHUNK == 0 else L
    xp, xs = x_prompt, x_sample
    kp_l, vp_l, ks_l, vs_l, sp_l, ss_l, cp_l, cs_l = [], [], [], [], [], [], [], []
    for l in range(DEPTH):
        lam_init = 0.8 - 0.6 * math.exp(-0.3 * l)
        lam = (jnp.exp(jnp.sum(lambda_q1[l].astype(F32) * lambda_k1[l].astype(F32)))
               - jnp.exp(jnp.sum(lambda_q2[l].astype(F32) * lambda_k2[l].astype(F32))) + lam_init)
        mod_p = adaln(c_prompt, w_ada[l], b_ada[l])
        mod_s = adaln(c_sample, w_ada[l], b_ada[l])
        layer_w = (w_in[l], b_in[l], conv_w[l], a_log[l], dt_bias[l], gdn_norm[l],
                   w_proj_attn[l], w_proj_gdn[l], w_out[l])

        attn_p = functools.partial(diff_attn_prompt, lam=lam, sub_g=attn_subln[l], lam_init=lam_init)
        hp = modulate(xp, norm_mix_pre[l], mod_p[0], mod_p[1])
        conv0 = jnp.zeros((B, CONV_W - 1, CONV_CH), xp.dtype)
        s0 = jnp.zeros((B, G_HEADS, G_DK, G_DV), F32)
        yp, kp, vp, cp, sp = mixer_sublayer(hp, attn_p, conv0, s0, chunk_p, *layer_w)
        xp = xp + mod_p[2] * rms_norm(yp, norm_mix_post[l])

        attn_s = functools.partial(diff_attn_sample, cache_k=cache_k, cache_v=cache_v, layer=l,
                                   page_table=page_table, lam=lam, sub_g=attn_subln[l], lam_init=lam_init)
        hs = modulate(xs, norm_mix_pre[l], mod_s[0], mod_s[1])
        ys, ksn, vsn, csn, ssn = mixer_sublayer(hs, attn_s, state_conv[l], state_ssm[l], S, *layer_w)
        xs = xs + mod_s[2] * rms_norm(ys, norm_mix_post[l])

        xp = xp + mod_p[5] * rms_norm(swiglu(modulate(xp, norm_ffn_pre[l], mod_p[3], mod_p[4]),
                                             w_ffn_up[l], w_ffn_down[l]), norm_ffn_post[l])
        xs = xs + mod_s[5] * rms_norm(swiglu(modulate(xs, norm_ffn_pre[l], mod_s[3], mod_s[4]),
                                             w_ffn_up[l], w_ffn_down[l]), norm_ffn_post[l])

        kp_l.append(kp)
        vp_l.append(vp)
        ks_l.append(ksn)
        vs_l.append(vsn)
        sp_l.append(sp.astype(state_ssm.dtype))
        ss_l.append(ssn.astype(state_ssm.dtype))
        cp_l.append(cp.astype(state_conv.dtype))
        cs_l.append(csn.astype(state_conv.dtype))

    k_prompt = jnp.stack(kp_l)
    v_prompt = jnp.stack(vp_l)
    k_sample = jnp.stack(ks_l)
    v_sample = jnp.stack(vs_l)
    ssm_prompt = jnp.stack(sp_l)
    ssm_sample = jnp.stack(ss_l)
    conv_prompt = jnp.stack(cp_l)
    conv_sample = jnp.stack(cs_l)
    return (xp, xs, k_prompt, v_prompt, k_sample, v_sample, ssm_prompt, ssm_sample, conv_prompt, conv_sample)
```

```python
import functools
import math

import jax
import jax.numpy as jnp
from jax import lax
from jax.experimental import pallas as pl
from jax.experimental.pallas import tpu as pltpu

F32 = jnp.float32
BF16 = jnp.bfloat16
EPS = 1e-6
NEG = -1e30
LANES = 128
VMEM_LIMIT = 56 * 1024 * 1024
HI = lax.Precision.HIGHEST


def _cparams(sem):
    return pltpu.CompilerParams(dimension_semantics=sem, vmem_limit_bytes=VMEM_LIMIT)


def _tile(n, target, mult=8):
    if n <= target:
        return n
    for t in range(target, 0, -1):
        if n % t == 0 and t % mult == 0:
            return t
    return n


def _silu(x):
    return x * jax.nn.sigmoid(x)


def _dot(a, b):
    return jnp.dot(a, b, preferred_element_type=F32)


def _dot_nt(a, b, precision=None):
    return lax.dot_general(a, b, (((1,), (1,)), ((), ())), preferred_element_type=F32,
                           precision=precision)


def _dot_tn(a, b):
    return lax.dot_general(a, b, (((0,), (0,)), ((), ())), preferred_element_type=F32)


def _adaln_kernel(c_ref, w_ref, b_ref, o_ref):
    a = _silu(c_ref[...]).astype(BF16)
    o_ref[...] = _dot(a, w_ref[...].astype(BF16)) + b_ref[...]


def _adaln(c, w, b):
    m, d = c.shape
    n = w.shape[1]
    tn = _tile(n, 1024, LANES)
    return pl.pallas_call(
        _adaln_kernel,
        grid=(n // tn,),
        in_specs=[pl.BlockSpec((m, d), lambda j: (0, 0)),
                  pl.BlockSpec((d, tn), lambda j: (0, j)),
                  pl.BlockSpec((1, tn), lambda j: (0, j))],
        out_specs=pl.BlockSpec((m, tn), lambda j: (0, j)),
        out_shape=jax.ShapeDtypeStruct((m, n), F32),
        compiler_params=_cparams(("arbitrary",)),
        name="adaln",
    )(c, w, b.reshape(1, n))


def _modulated(x, g, shift, scale):
    r = lax.rsqrt(jnp.mean(x * x, axis=-1, keepdims=True) + EPS)
    return ((x * r) * g) * (1.0 + scale) + shift


def _inproj_kernel(x_ref, g_ref, sh_ref, sc_ref, w_ref, b_ref, ws_ref, bs_ref,
                   oattn_ref, okv_ref, ogdn_ref, ozg_ref, osm_ref, h_ref):
    j = pl.program_id(1)

    @pl.when(j == 0)
    def _():
        hb = _modulated(x_ref[...], g_ref[...], sh_ref[...], sc_ref[...]).astype(BF16)
        h_ref[...] = hb
        osm_ref[...] = _dot(hb, ws_ref[...]) + bs_ref[...]

    y = _dot(h_ref[...], w_ref[...]) + b_ref[...]

    @pl.when(j < 3)
    def _():
        oattn_ref[...] = y.astype(BF16)

    @pl.when((j == 1) | (j == 2))
    def _():
        okv_ref[...] = y

    @pl.when((j >= 3) & (j < 6))
    def _():
        ogdn_ref[...] = y

    @pl.when(j >= 6)
    def _():
        ozg_ref[...] = y.astype(BF16)


def _mod_spec(rows_per_seq, tm, d, chunk, per_row):
    if per_row:
        return pl.BlockSpec((tm, d), lambda i, *_: (i, chunk))
    return pl.BlockSpec((None, 1, d), lambda i, *_: ((i * tm) // rows_per_seq, 0, chunk))


def _inproj(x, mod, rows_per_seq, per_row, g, w_main, b_main, w_small, b_small, width):
    r, d = x.shape
    n = w_main.shape[1]
    tn = width
    nj = n // tn
    tm = _tile(rows_per_seq if not per_row else r, 512)
    grid = (r // tm, nj)
    clip = lambda j, lo, hi: jnp.minimum(jnp.maximum(j - lo, 0), hi - lo)
    return pl.pallas_call(
        _inproj_kernel,
        grid=grid,
        in_specs=[pl.BlockSpec((tm, d), lambda i, j: (i, 0)),
                  pl.BlockSpec((1, d), lambda i, j: (0, 0)),
                  _mod_spec(rows_per_seq, tm, d, 0, per_row),
                  _mod_spec(rows_per_seq, tm, d, 1, per_row),
                  pl.BlockSpec((d, tn), lambda i, j: (0, j)),
                  pl.BlockSpec((1, tn), lambda i, j: (0, j)),
                  pl.BlockSpec((d, LANES), lambda i, j: (0, 0)),
                  pl.BlockSpec((1, LANES), lambda i, j: (0, 0))],
        out_specs=[pl.BlockSpec((tm, tn), lambda i, j: (i, clip(j, 0, 2))),
                   pl.BlockSpec((tm, tn), lambda i, j: (i, clip(j, 1, 2))),
                   pl.BlockSpec((tm, tn), lambda i, j: (i, clip(j, 3, 5))),
                   pl.BlockSpec((tm, tn), lambda i, j: (i, clip(j, 6, nj - 1))),
                   pl.BlockSpec((tm, LANES), lambda i, j: (i, 0))],
        out_shape=[jax.ShapeDtypeStruct((r, 3 * tn), BF16),
                   jax.ShapeDtypeStruct((r, 2 * tn), F32),
                   jax.ShapeDtypeStruct((r, 3 * tn), F32),
                   jax.ShapeDtypeStruct((r, (nj - 6) * tn), BF16),
                   jax.ShapeDtypeStruct((r, LANES), F32)],
        scratch_shapes=[pltpu.VMEM((tm, d), BF16)],
        compiler_params=_cparams(("arbitrary", "arbitrary")),
        name="inproj",
    )(x, g, mod, mod, w_main, b_main, w_small, b_small)


def _lambda_value(lq1, lk1, lq2, lk2, lam_init):
    s1 = jnp.sum(lq1 * lk1, axis=-1, keepdims=True)
    s2 = jnp.sum(lq2 * lk2, axis=-1, keepdims=True)
    return jnp.exp(s1) - jnp.exp(s2) + lam_init


def _diff_finish(acc, l, lam, subg, lam_init, half):
    o1 = acc[:half] / l[:half]
    o2 = acc[half:] / l[half:]
    a = o1 - lam * o2
    r = lax.rsqrt(jnp.mean(a * a, axis=-1, keepdims=True) + EPS)
    return ((a * r) * subg) * (1.0 - lam_init)


def _split_maps(q, scale):
    lane = lax.broadcasted_iota(jnp.int32, q.shape, 1)
    qs = q * scale
    zero = jnp.zeros_like(qs)
    half = q.shape[-1] // 2
    return jnp.concatenate([jnp.where(lane < half, qs, zero), jnp.where(lane >= half, qs, zero)], axis=0)


def _pattn_kernel(slopes_ref, lq1_ref, lk1_ref, lq2_ref, lk2_ref, subg_ref, q_ref, k_ref, v_ref,
                  o_ref, m_ref, l_ref, acc_ref, *, tq, lam_init, scale):
    h = pl.program_id(1)
    qi = pl.program_id(2)
    slope = slopes_ref[h]
    qq = _split_maps(q_ref[...], scale)
    col = lax.broadcasted_iota(jnp.int32, (1, tq), 1).astype(F32)
    colbias = slope * col
    m_ref[...] = jnp.full(m_ref.shape, NEG, F32)
    l_ref[...] = jnp.zeros(l_ref.shape, F32)
    acc_ref[...] = jnp.zeros(acc_ref.shape, F32)
    step = slope * tq

    def chunk(c, masked):
        start = pl.multiple_of(c * tq, tq)
        k = k_ref[pl.ds(start, tq), :]
        v = v_ref[pl.ds(start, tq), :]
        s = _dot_nt(qq, k) + colbias
        if masked:
            row = lax.broadcasted_iota(jnp.int32, (2 * tq, tq), 0)
            cc = lax.broadcasted_iota(jnp.int32, (2 * tq, tq), 1)
            row = jnp.where(row >= tq, row - tq, row)
            s = jnp.where(row >= cc, s, NEG)
        m_prev = m_ref[...] - step
        m_new = jnp.maximum(m_prev, jnp.max(s, axis=-1, keepdims=True))
        alpha = jnp.exp(m_prev - m_new)
        p = jnp.exp(s - m_new)
        l_ref[...] = alpha * l_ref[...] + jnp.sum(p, axis=-1, keepdims=True)
        acc_ref[...] = alpha * acc_ref[...] + _dot(p.astype(BF16), v)
        m_ref[...] = m_new

    def body(c, carry):
        chunk(c, False)
        return carry

    lax.fori_loop(0, qi, body, 0)
    chunk(qi, True)
    lam = _lambda_value(lq1_ref[...], lk1_ref[...], lq2_ref[...], lk2_ref[...], lam_init)
    o_ref[...] = _diff_finish(acc_ref[...], l_ref[...], lam, subg_ref[...], lam_init, tq).astype(o_ref.dtype)


def _prompt_attention(qkv, slopes, lam_vecs, subg, n_heads, lam_init):
    b, l, _ = qkv.shape
    tq = _tile(l, 256)
    hd = LANES
    vec = lambda n: pl.BlockSpec((1, n), lambda b_, h, i: (0, 0))
    kern = functools.partial(_pattn_kernel, tq=tq, lam_init=lam_init, scale=(hd // 2) ** -0.5)
    return pl.pallas_call(
        kern,
        grid=(b, n_heads, l // tq),
        in_specs=[pl.BlockSpec(memory_space=pltpu.SMEM),
                  vec(hd // 2), vec(hd // 2), vec(hd // 2), vec(hd // 2), vec(hd),
                  pl.BlockSpec((None, tq, hd), lambda b_, h, i: (b_, i, h)),
                  pl.BlockSpec((None, l, hd), lambda b_, h, i: (b_, 0, n_heads + h)),
                  pl.BlockSpec((None, l, hd), lambda b_, h, i: (b_, 0, 2 * n_heads + h))],
        out_specs=pl.BlockSpec((None, tq, hd), lambda b_, h, i: (b_, i, h)),
        out_shape=jax.ShapeDtypeStruct((b, l, n_heads * hd), BF16),
        scratch_shapes=[pltpu.VMEM((2 * tq, 1), F32), pltpu.VMEM((2 * tq, 1), F32),
                        pltpu.VMEM((2 * tq, hd), F32)],
        compiler_params=_cparams(("arbitrary", "arbitrary", "arbitrary")),
        name="prompt_attn",
    )(slopes, *lam_vecs, subg, qkv, qkv, qkv)


def _sattn_kernel(pt_ref, slope_ref, bias_ref, biasn_ref, lq1_ref, lk1_ref, lq2_ref, lk2_ref, subg_ref,
                  q_ref, kn_ref, vn_ref, *rest, n_pages_step, page, lam_init):
    k_refs = rest[:n_pages_step]
    v_refs = rest[n_pages_step:2 * n_pages_step]
    o_ref, m_ref, l_ref, acc_ref = rest[2 * n_pages_step:]
    g = pl.program_id(1)
    ng = pl.num_programs(1)

    @pl.when(g == 0)
    def _():
        m_ref[...] = jnp.full(m_ref.shape, NEG, F32)
        l_ref[...] = jnp.zeros(l_ref.shape, F32)
        acc_ref[...] = jnp.zeros(acc_ref.shape, F32)

    qq = q_ref[...]
    step = slope_ref[...] * page

    def update(kf, vf, bias):
        s = _dot_nt(qq, kf) + bias
        m_prev = m_ref[...] - step
        m_new = jnp.maximum(m_prev, jnp.max(s, axis=-1, keepdims=True))
        alpha = jnp.exp(m_prev - m_new)
        p = jnp.exp(s - m_new)
        l_ref[...] = alpha * l_ref[...] + jnp.sum(p, axis=-1, keepdims=True)
        acc_ref[...] = alpha * acc_ref[...] + _dot(p.astype(BF16), vf)
        m_ref[...] = m_new

    for i in range(n_pages_step):
        kf = k_refs[i][...].reshape(-1, LANES).astype(BF16)
        vf = v_refs[i][...].reshape(-1, LANES).astype(BF16)
        update(kf, vf, bias_ref[...])

    @pl.when(g == ng - 1)
    def _():
        update(kn_ref[...], vn_ref[...], biasn_ref[...])
        lam = _lambda_value(lq1_ref[...], lk1_ref[...], lq2_ref[...], lk2_ref[...], lam_init)
        half = acc_ref.shape[0] // 2
        o_ref[...] = _diff_finish(acc_ref[...], l_ref[...], lam, subg_ref[...], lam_init, half)


def _sample_attention(qrows, knew, vnew, cache_k, cache_v, layer, page_table, slopes, lam_vecs, subg, lam_init):
    db, nrow, hd = qrows.shape
    _, n_phys, page, n_heads, _ = cache_k.shape
    s_new = knew.shape[1] // n_heads
    n_pages = page_table.shape[1]
    pps = 8 if n_pages % 8 == 0 else 1
    r = jnp.arange(nrow)
    r_head = (r // s_new) % n_heads
    r_tok = r % s_new
    slope_rows = slopes[r_head][:, None]
    c = jnp.arange(page * n_heads)
    bias = jnp.where(r_head[:, None] == (c % n_heads)[None, :], slope_rows * (c // n_heads)[None, :].astype(F32), NEG)
    cn = jnp.arange(s_new * n_heads)
    ok = (r_head[:, None] == (cn % n_heads)[None, :]) & ((cn // n_heads)[None, :] <= r_tok[:, None])
    biasn = jnp.where(ok, slope_rows * (cn // n_heads)[None, :].astype(F32), NEG)

    const = lambda shape: pl.BlockSpec(shape, lambda b, g, pt: (0,) * len(shape))
    page_spec = lambda i: pl.BlockSpec((None, None, page, n_heads, hd),
                                       lambda b, g, pt: (layer, pt[b, g * pps + i], 0, 0, 0))
    kern = functools.partial(_sattn_kernel, n_pages_step=pps, page=float(page), lam_init=lam_init)
    grid_spec = pltpu.PrefetchScalarGridSpec(
        num_scalar_prefetch=1,
        grid=(db, n_pages // pps),
        in_specs=[const((nrow, 1)), const((nrow, page * n_heads)), const((nrow, s_new * n_heads)),
                  const((1, hd // 2)), const((1, hd // 2)), const((1, hd // 2)), const((1, hd // 2)), const((1, hd)),
                  pl.BlockSpec((None, nrow, hd), lambda b, g, pt: (b, 0, 0)),
                  pl.BlockSpec((None, s_new * n_heads, hd), lambda b, g, pt: (b, 0, 0)),
                  pl.BlockSpec((None, s_new * n_heads, hd), lambda b, g, pt: (b, 0, 0))]
                 + [page_spec(i) for i in range(pps)] + [page_spec(i) for i in range(pps)],
        out_specs=pl.BlockSpec((None, nrow // 2, hd), lambda b, g, pt: (b, 0, 0)),
        scratch_shapes=[pltpu.VMEM((nrow, 1), F32), pltpu.VMEM((nrow, 1), F32), pltpu.VMEM((nrow, hd), F32)],
    )
    return pl.pallas_call(
        kern,
        grid_spec=grid_spec,
        out_shape=jax.ShapeDtypeStruct((db, nrow // 2, hd), F32),
        compiler_params=_cparams(("arbitrary", "arbitrary")),
        name="sample_attn",
    )(page_table, slope_rows, bias, biasn, *lam_vecs, subg, qrows, knew, vnew,
      *([cache_k] * pps), *([cache_v] * pps))


def _gdn_prep_kernel(x_ref, sm_ref, st_ref, cw_ref, alog_ref, dtb_ref, q_ref, k_ref, v_ref, gate_ref, pad_ref,
                     *, tt, n_heads, valid_len, taps):
    t = pl.program_id(1)
    halo = 8
    w3 = n_heads * LANES

    @pl.when(t == 0)
    def _():
        pad_ref[0:halo, :] = jnp.zeros((halo, pad_ref.shape[1]), F32)
        pad_ref[halo - (taps - 1):halo, :] = st_ref[...]

    pad_ref[halo:halo + tt, :] = x_ref[...]
    y = pad_ref[halo:halo + tt, :] * cw_ref[taps - 1:taps, :]
    for j in range(taps - 1):
        off = halo - (taps - 1) + j
        y = y + pad_ref[off:off + tt, :] * cw_ref[j:j + 1, :]
    y = _silu(y)
    pad_ref[0:halo, :] = pad_ref[tt:tt + halo, :]

    def l2n(z):
        return z * lax.rsqrt(jnp.sum(z * z, axis=-1, keepdims=True) + EPS)

    for hh in range(n_heads):
        sl = slice(hh * LANES, (hh + 1) * LANES)
        q_ref[:, sl] = l2n(y[:, sl]) * (LANES ** -0.5)
        k_ref[:, sl] = l2n(y[:, w3 + hh * LANES:w3 + (hh + 1) * LANES])
    v_ref[...] = y[:, 2 * w3:]

    x = sm_ref[...]
    lane = lax.broadcasted_iota(jnp.int32, x.shape, 1)
    row = lax.broadcasted_iota(jnp.int32, x.shape, 0) + t * tt
    beta = jax.nn.sigmoid(x)
    z = x + dtb_ref[...]
    softplus = jnp.maximum(z, 0.0) + jnp.log(1.0 + jnp.exp(-jnp.abs(z)))
    gdec = -jnp.exp(alog_ref[...]) * softplus
    out = jnp.where(lane < n_heads, beta, gdec)
    gate_ref[...] = jnp.where((row < valid_len) & (lane < 2 * n_heads), out, 0.0)


def _gdn_prep(x, small, state, conv_w, a_log, dt_bias, n_heads, valid_len):
    b, l, ch = x.shape
    taps = conv_w.shape[0]
    tt = _tile(l, 256)
    w = n_heads * LANES
    lanes = jnp.arange(LANES)
    in_g = (lanes >= n_heads) & (lanes < 2 * n_heads)
    alog_v = jnp.where(in_g, a_log[jnp.clip(lanes - n_heads, 0, n_heads - 1)], 0.0).reshape(1, LANES)
    dtb_v = jnp.where(in_g, dt_bias[jnp.clip(lanes - n_heads, 0, n_heads - 1)], 0.0).reshape(1, LANES)
    kern = functools.partial(_gdn_prep_kernel, tt=tt, n_heads=n_heads, valid_len=valid_len, taps=taps)
    row_spec = lambda n: pl.BlockSpec((None, tt, n), lambda b_, t: (b_, t, 0))
    return pl.pallas_call(
        kern,
        grid=(b, l // tt),
        in_specs=[row_spec(ch), row_spec(LANES),
                  pl.BlockSpec((None, taps - 1, ch), lambda b_, t: (b_, 0, 0)),
                  pl.BlockSpec((taps, ch), lambda b_, t: (0, 0)),
                  pl.BlockSpec((1, LANES), lambda b_, t: (0, 0)),
                  pl.BlockSpec((1, LANES), lambda b_, t: (0, 0))],
        out_specs=[row_spec(w), row_spec(w), row_spec(w), row_spec(LANES)],
        out_shape=[jax.ShapeDtypeStruct((b, l, w), F32)] * 3 + [jax.ShapeDtypeStruct((b, l, LANES), F32)],
        scratch_shapes=[pltpu.VMEM((tt + 8, ch), F32)],
        compiler_params=_cparams(("arbitrary", "arbitrary")),
        name="gdn_prep",
    )(x, small, state, conv_w, alog_v, dtb_v)


def _gdn_kernel(q_ref, k_ref, v_ref, gate_ref, z_ref, gn_ref, s0_ref, o_ref, s_ref, *, chunk, n_chunks, n_heads):
    t = pl.program_id(1)

    @pl.when(t == 0)
    def _():
        s_ref[...] = s0_ref[...]

    c_ = chunk
    ri = lax.broadcasted_iota(jnp.int32, (c_, c_), 0)
    ci = lax.broadcasted_iota(jnp.int32, (c_, c_), 1)
    causal = ri >= ci
    strict = ri > ci
    tri = jnp.where(causal, 1.0, 0.0).astype(F32)
    eye = jnp.where(ri == ci, 1.0, 0.0).astype(F32)
    n_sq = max(int(math.log2(c_)) - 1, 0)
    gn = gn_ref[...]
    sel_r = lax.broadcasted_iota(jnp.int32, (8, LANES), 0)
    sel_c = lax.broadcasted_iota(jnp.int32, (8, LANES), 1)
    sel = jnp.where(sel_c == sel_r + n_heads, 1.0, 0.0).astype(F32)
    zpad = jnp.zeros((LANES - c_, LANES), F32)

    def one_chunk(c, carry):
        rows = pl.ds(c * c_ if n_chunks == 1 else pl.multiple_of(c * c_, c_), c_)
        gates = gate_ref[rows, :]
        gcum = jnp.dot(tri, gates, preferred_element_type=F32, precision=HI)
        gcum_t = _dot_nt(sel, gcum, precision=HI)
        for hh in range(n_heads):
            sl = slice(hh * LANES, (hh + 1) * LANES)
            q = q_ref[rows, sl]
            k = k_ref[rows, sl]
            v = v_ref[rows, sl]
            beta = gates[:, hh:hh + 1]
            gc = gcum[:, n_heads + hh:n_heads + hh + 1]
            gr = gcum_t[hh:hh + 1, :]
            glast = gcum[c_ - 1:c_, n_heads + hh:n_heads + hh + 1]
            decay = jnp.exp(jnp.where(causal, gc - gr, NEG))
            qk_kk = _dot_nt(jnp.concatenate([k, q], axis=0), k, precision=HI)
            a = jnp.where(strict, beta * qk_kk[:c_] * decay, 0.0)
            tinv = eye - a
            ap = a
            for _ in range(n_sq):
                ap = jnp.dot(ap, ap, preferred_element_type=F32, precision=HI)
                tinv = tinv + jnp.dot(tinv, ap, preferred_element_type=F32, precision=HI)
            eg = jnp.exp(gc)
            rhs = jnp.concatenate([v * beta, k * (beta * eg)], axis=1).astype(BF16)
            uw = _dot(tinv.astype(BF16), rhs)
            s = s_ref[hh]
            sb = s.astype(BF16)
            wq = _dot(jnp.concatenate([uw[:, LANES:], q * eg], axis=0).astype(BF16), sb)
            u = uw[:, :LANES] - wq[:c_]
            ub = u.astype(BF16)
            o = wq[c_:] + _dot((qk_kk[c_:] * decay).astype(BF16), ub)
            kd = k * jnp.exp(glast - gc)
            kd_t = jnp.concatenate([kd, zpad], axis=0).T.astype(BF16)
            u_pad = jnp.concatenate([u, zpad], axis=0).astype(BF16)
            s_ref[hh] = s * jnp.exp(glast) + _dot(kd_t, u_pad)
            r = lax.rsqrt(jnp.mean(o * o, axis=-1, keepdims=True) + EPS)
            o_ref[rows, sl] = (((o * r) * gn) * _silu(z_ref[rows, sl].astype(F32))).astype(o_ref.dtype)
        return carry

    if n_chunks == 1:
        one_chunk(0, 0)
    else:
        lax.fori_loop(0, n_chunks, one_chunk, 0)


def _gdn(q, k, v, gates, z_arr, z_block, gn, s0, chunk, n_heads):
    b, l, w = q.shape
    tt = _tile(l, max(chunk, 256), chunk)
    kern = functools.partial(_gdn_kernel, chunk=chunk, n_chunks=tt // chunk, n_heads=n_heads)
    row_spec = lambda n: pl.BlockSpec((None, tt, n), lambda b_, t: (b_, t, 0))
    st_spec = pl.BlockSpec((None, n_heads, LANES, LANES), lambda b_, t: (b_, 0, 0, 0))
    return pl.pallas_call(
        kern,
        grid=(b, l // tt),
        in_specs=[row_spec(w), row_spec(w), row_spec(w), row_spec(LANES),
                  pl.BlockSpec((None, tt, w), lambda b_, t: (b_, t, z_block)),
                  pl.BlockSpec((1, LANES), lambda b_, t: (0, 0)),
                  st_spec],
        out_specs=[row_spec(w), st_spec],
        out_shape=[jax.ShapeDtypeStruct((b, l, w), BF16),
                   jax.ShapeDtypeStruct(s0.shape, F32)],
        compiler_params=_cparams(("arbitrary", "arbitrary")),
        name="gdn",
    )(q, k, v, gates, z_arr, gn, s0)


def _merge_kernel(a_ref, g_ref, ga_ref, gg_ref, wpa_ref, wpg_ref, wo_ref, x_ref, gate_ref, gn_ref, o_ref):
    pa = _dot(a_ref[...], wpa_ref[...])
    pg = _dot(g_ref[...], wpg_ref[...])
    m = jax.nn.sigmoid(ga_ref[...].astype(F32)) * pa + jax.nn.sigmoid(gg_ref[...].astype(F32)) * pg
    y = _dot(m.astype(BF16), wo_ref[...])
    r = lax.rsqrt(jnp.mean(y * y, axis=-1, keepdims=True) + EPS)
    o_ref[...] = x_ref[...] + gate_ref[...] * ((y * r) * gn_ref[...])


def _merge(a_out, g_out, zg, x, mod, rows_per_seq, per_row, w_pa, w_pg, w_o, gn):
    r, d = x.shape
    w = a_out.shape[1]
    tm = _tile(rows_per_seq if not per_row else r, 256)
    resident = lambda shape: pl.BlockSpec(shape, lambda i: (0, 0), pipeline_mode=pl.Buffered(1))
    return pl.pallas_call(
        _merge_kernel,
        grid=(r // tm,),
        in_specs=[pl.BlockSpec((tm, w), lambda i: (i, 0)),
                  pl.BlockSpec((tm, w), lambda i: (i, 0)),
                  pl.BlockSpec((tm, d), lambda i: (i, 0)),
                  pl.BlockSpec((tm, d), lambda i: (i, 1)),
                  resident((w, d)), resident((w, d)), resident((d, d)),
                  pl.BlockSpec((tm, d), lambda i: (i, 0)),
                  _mod_spec(rows_per_seq, tm, d, 2, per_row),
                  pl.BlockSpec((1, d), lambda i: (0, 0))],
        out_specs=pl.BlockSpec((tm, d), lambda i: (i, 0)),
        out_shape=jax.ShapeDtypeStruct((r, d), F32),
        compiler_params=_cparams(("arbitrary",)),
        name="merge",
    )(a_out, g_out, zg, zg, w_pa, w_pg, w_o, x, mod, gn)


def _ffn_up_kernel(x_ref, g_ref, sh_ref, sc_ref, wg_ref, wu_ref, o_ref, h_ref):
    @pl.when(pl.program_id(1) == 0)
    def _():
        h_ref[...] = _modulated(x_ref[...], g_ref[...], sh_ref[...], sc_ref[...]).astype(BF16)

    h = h_ref[...]
    gt = _dot(h, wg_ref[...])
    up = _dot(h, wu_ref[...])
    o_ref[...] = (_silu(gt) * up).astype(o_ref.dtype)


def _ffn_up(x, mod, rows_per_seq, per_row, g, w_up):
    r, d = x.shape
    f = w_up.shape[1] // 2
    tn = _tile(f, 512, LANES)
    nj = f // tn
    tm = _tile(rows_per_seq if not per_row else r, 512)
    return pl.pallas_call(
        _ffn_up_kernel,
        grid=(r // tm, nj),
        in_specs=[pl.BlockSpec((tm, d), lambda i, j: (i, 0)),
                  pl.BlockSpec((1, d), lambda i, j: (0, 0)),
                  _mod_spec(rows_per_seq, tm, d, 3, per_row),
                  _mod_spec(rows_per_seq, tm, d, 4, per_row),
                  pl.BlockSpec((d, tn), lambda i, j: (0, j)),
                  pl.BlockSpec((d, tn), lambda i, j: (0, nj + j))],
        out_specs=pl.BlockSpec((tm, tn), lambda i, j: (i, j)),
        out_shape=jax.ShapeDtypeStruct((r, f), BF16),
        scratch_shapes=[pltpu.VMEM((tm, d), BF16)],
        compiler_params=_cparams(("arbitrary", "arbitrary")),
        name="ffn_up",
    )(x, g, mod, mod, w_up, w_up)


def _ffn_down_kernel(a_ref, w_ref, x_ref, gate_ref, gn_ref, o_ref):
    y = _dot(a_ref[...], w_ref[...])
    r = lax.rsqrt(jnp.mean(y * y, axis=-1, keepdims=True) + EPS)
    o_ref[...] = x_ref[...] + gate_ref[...] * ((y * r) * gn_ref[...])


def _ffn_down(act, x, mod, rows_per_seq, per_row, w_down, gn):
    r, d = x.shape
    f = act.shape[1]
    tm = _tile(rows_per_seq if not per_row else r, 256)
    return pl.pallas_call(
        _ffn_down_kernel,
        grid=(r // tm,),
        in_specs=[pl.BlockSpec((tm, f), lambda i: (i, 0)),
                  pl.BlockSpec((f, d), lambda i: (0, 0), pipeline_mode=pl.Buffered(1)),
                  pl.BlockSpec((tm, d), lambda i: (i, 0)),
                  _mod_spec(rows_per_seq, tm, d, 5, per_row),
                  pl.BlockSpec((1, d), lambda i: (0, 0))],
        out_specs=pl.BlockSpec((tm, d), lambda i: (i, 0)),
        out_shape=jax.ShapeDtypeStruct((r, d), F32),
        compiler_params=_cparams(("arbitrary",)),
        name="ffn_down",
    )(act, w_down, x, mod, gn)


def _group(x, mod, per_row, attn_fn, conv_state, s0, chunk, valid_len, pw, n_heads):
    b, l, d = x.shape
    r = b * l
    w = n_heads * LANES
    xf = x.reshape(r, d)
    oattn, okv, ogdn, ozg, osm = _inproj(xf, mod, l, per_row, pw["g_mix_pre"], pw["w_main"], pw["b_main"],
                                         pw["w_small"], pw["b_small"], w)
    a_out = attn_fn(oattn)
    q, k, v, gates = _gdn_prep(ogdn.reshape(b, l, 3 * w), osm.reshape(b, l, LANES), conv_state,
                               pw["conv_w"], pw["a_log"], pw["dt_bias"], n_heads, valid_len)
    g_out, s_new = _gdn(q, k, v, gates, ozg.reshape(b, l, -1), (2 * d) // w, pw["gdn_norm"], s0, chunk, n_heads)
    x1 = _merge(a_out, g_out.reshape(r, w), ozg, xf, mod, l, per_row, pw["w_pa"], pw["w_pg"], pw["w_o"],
                pw["g_mix_post"])
    act = _ffn_up(x1, mod, l, per_row, pw["g_ffn_pre"], pw["w_up"])
    x2 = _ffn_down(act, x1, mod, l, per_row, pw["w_down"], pw["g_ffn_post"])
    return x2.reshape(b, l, d), okv, ogdn, s_new


def kernel(x_prompt, x_sample, c_prompt, c_sample, cache_k, cache_v, page_table, state_ssm, state_conv, w_ada, b_ada, norm_mix_pre, norm_mix_post, norm_ffn_pre, norm_ffn_post, w_in, b_in, conv_w, lambda_q1, lambda_k1, lambda_q2, lambda_k2, attn_subln, a_log, dt_bias, gdn_norm, w_proj_attn, w_proj_gdn, w_out, w_ffn_up, w_ffn_down):
    bsz, seq, d = x_prompt.shape
    db, s_new, _ = x_sample.shape
    depth = w_in.shape[0]
    n_heads = cache_k.shape[3]
    hd = cache_k.shape[4]
    assert hd == LANES and state_ssm.shape[-1] == LANES and state_ssm.shape[-2] == LANES
    assert state_ssm.shape[2] == n_heads
    w = n_heads * hd
    taps = conv_w.shape[1]
    chunk_p = 64 if seq % 64 == 0 else seq
    slopes = 2.0 ** (-8.0 * jnp.arange(1, n_heads + 1, dtype=F32) / n_heads)
    assert (2 * d) % w == 0
    s_pad = -(-s_new // 16) * 16

    xp, xs = x_prompt, x_sample
    outs = [[] for _ in range(8)]
    for l in range(depth):
        lam_init = 0.8 - 0.6 * math.exp(-0.3 * l)
        lam_vecs = [t[l].reshape(1, -1) for t in (lambda_q1, lambda_k1, lambda_q2, lambda_k2)]
        subg = attn_subln[l].reshape(1, hd)

        c_all = jnp.concatenate([c_prompt, c_sample], axis=0)
        c_all = jnp.pad(c_all, ((0, (-c_all.shape[0]) % 8), (0, 0)))
        mod = _adaln(c_all, w_ada[l], b_ada[l])
        mod_p = mod[:bsz].reshape(bsz, 1, 6 * d)
        mod_s = jnp.repeat(mod[bsz:bsz + db], s_pad, axis=0)

        wi, bi = w_in[l], b_in[l]
        n6 = 6 * w
        n7 = 7 * w
        pw = dict(
            w_main=jnp.concatenate([wi[:, :n6], wi[:, n7 + 2 * n_heads:], wi[:, n6:n7]], axis=1).astype(BF16),
            b_main=jnp.concatenate([bi[:n6], bi[n7 + 2 * n_heads:], bi[n6:n7]]).reshape(1, -1),
            w_small=jnp.pad(wi[:, n7:n7 + 2 * n_heads], ((0, 0), (0, LANES - 2 * n_heads))).astype(BF16),
            b_small=jnp.pad(bi[n7:n7 + 2 * n_heads], (0, LANES - 2 * n_heads)).reshape(1, LANES),
            g_mix_pre=norm_mix_pre[l].reshape(1, d), g_mix_post=norm_mix_post[l].reshape(1, d),
            g_ffn_pre=norm_ffn_pre[l].reshape(1, d), g_ffn_post=norm_ffn_post[l].reshape(1, d),
            conv_w=conv_w[l], a_log=a_log[l], dt_bias=dt_bias[l], gdn_norm=gdn_norm[l].reshape(1, hd),
            w_pa=w_proj_attn[l].astype(BF16), w_pg=w_proj_gdn[l].astype(BF16), w_o=w_out[l].astype(BF16),
            w_up=w_ffn_up[l].astype(BF16), w_down=w_ffn_down[l].astype(BF16),
        )

        def attn_p(oattn):
            o = _prompt_attention(oattn.reshape(bsz, seq, 3 * w), slopes, lam_vecs, subg, n_heads, lam_init)
            return o.reshape(bsz * seq, w)

        conv0 = jnp.zeros((bsz, taps - 1, 3 * w), F32)
        s0 = jnp.zeros((bsz, n_heads, hd, hd), F32)
        xp, kv_p, raw_p, ssm_p = _group(xp, mod_p, False, attn_p, conv0, s0, chunk_p, seq, pw, n_heads)

        xs_pad = jnp.pad(xs, ((0, 0), (0, s_pad - s_new), (0, 0)))

        def attn_s(oattn):
            qkv = oattn.reshape(db, s_pad, 3, n_heads, hd)[:, :s_new]
            qh = qkv[:, :, 0].transpose(0, 2, 1, 3)
            qrows = jax.vmap(lambda t: _split_maps(t, (hd // 2) ** -0.5))(qh.reshape(db, n_heads * s_new, hd))
            knew = qkv[:, :, 1].reshape(db, s_new * n_heads, hd)
            vnew = qkv[:, :, 2].reshape(db, s_new * n_heads, hd)
            o = _sample_attention(qrows, knew, vnew, cache_k, cache_v, l, page_table, slopes, lam_vecs, subg,
                                  lam_init)
            o = o.reshape(db, n_heads, s_new, hd).transpose(0, 2, 1, 3).reshape(db, s_new, w)
            return jnp.pad(o, ((0, 0), (0, s_pad - s_new), (0, 0))).reshape(db * s_pad, w).astype(BF16)

        xs_full, kv_s, raw_s, ssm_s = _group(xs_pad, mod_s, True, attn_s, state_conv[l], state_ssm[l], s_pad,
                                             s_new, pw, n_heads)
        xs = xs_full[:, :s_new]

        kv_p = kv_p.reshape(bsz, seq, 2, n_heads, hd)
        kv_s = kv_s.reshape(db, s_pad, 2, n_heads, hd)[:, :s_new]
        raw_p = raw_p.reshape(bsz, seq, 3 * w)
        raw_s = jnp.concatenate([state_conv[l], raw_s.reshape(db, s_pad, 3 * w)[:, :s_new]], axis=1)
        for lst, val in zip(outs, (kv_p[:, :, 0], kv_p[:, :, 1], kv_s[:, :, 0], kv_s[:, :, 1], ssm_p, ssm_s,
                                   raw_p[:, seq - (taps - 1):], raw_s[:, s_new:])):
            lst.append(val)

    return (xp, xs) + tuple(jnp.stack(o) for o in outs)
```

```python
import functools
import math

import jax
import jax.numpy as jnp
from jax import lax
from jax.experimental import pallas as pl
from jax.experimental.pallas import tpu as pltpu

F32 = jnp.float32
BF16 = jnp.bfloat16
EPS = 1e-6
NEG = -1e30
LANES = 128
VMEM_LIMIT = 56 * 1024 * 1024
HI = lax.Precision.HIGHEST


def _cparams(sem):
    return pltpu.CompilerParams(dimension_semantics=sem, vmem_limit_bytes=VMEM_LIMIT)


def _tile(n, target, mult=8):
    if n <= target:
        return n
    for t in range(target, 0, -1):
        if n % t == 0 and t % mult == 0:
            return t
    return n


def _silu(x):
    return x * jax.nn.sigmoid(x)


def _dot(a, b):
    return jnp.dot(a, b, preferred_element_type=F32)


def _dot_nt(a, b, precision=None):
    return lax.dot_general(a, b, (((1,), (1,)), ((), ())), preferred_element_type=F32,
                           precision=precision)


def _dot_tn(a, b):
    return lax.dot_general(a, b, (((0,), (0,)), ((), ())), preferred_element_type=F32)


def _adaln_kernel(c_ref, w_ref, b_ref, o_ref):
    a = _silu(c_ref[...]).astype(BF16)
    o_ref[...] = _dot(a, w_ref[...].astype(BF16)) + b_ref[...]


def _adaln(c, w, b):
    m, d = c.shape
    n = w.shape[1]
    tn = _tile(n, 1024, LANES)
    return pl.pallas_call(
        _adaln_kernel,
        grid=(n // tn,),
        in_specs=[pl.BlockSpec((m, d), lambda j: (0, 0)),
                  pl.BlockSpec((d, tn), lambda j: (0, j)),
                  pl.BlockSpec((1, tn), lambda j: (0, j))],
        out_specs=pl.BlockSpec((m, tn), lambda j: (0, j)),
        out_shape=jax.ShapeDtypeStruct((m, n), F32),
        compiler_params=_cparams(("arbitrary",)),
        name="adaln",
    )(c, w, b.reshape(1, n))


def _modulated(x, g, shift, scale):
    r = lax.rsqrt(jnp.mean(x * x, axis=-1, keepdims=True) + EPS)
    return ((x * r) * g) * (1.0 + scale) + shift


def _inproj_kernel(x_ref, g_ref, sh_ref, sc_ref, w_ref, b_ref, ws_ref, bs_ref,
                   oattn_ref, ok_ref, ov_ref, ogdn_ref, ozg_ref, osm_ref, h_ref):
    j = pl.program_id(1)

    @pl.when(j == 0)
    def _():
        hb = _modulated(x_ref[...], g_ref[...], sh_ref[...], sc_ref[...]).astype(BF16)
        h_ref[...] = hb
        osm_ref[...] = _dot_nt(hb, ws_ref[...]) + bs_ref[...]

    y = _dot_nt(h_ref[...], w_ref[...]) + b_ref[...]

    @pl.when(j < 3)
    def _():
        oattn_ref[...] = y.astype(BF16)

    def heads_to_sublanes(o_ref):
        for hh in range(o_ref.shape[1]):
            o_ref[:, hh, :] = y[:, hh * LANES:(hh + 1) * LANES]

    @pl.when(j == 1)
    def _():
        heads_to_sublanes(ok_ref)

    @pl.when(j == 2)
    def _():
        heads_to_sublanes(ov_ref)

    @pl.when((j >= 3) & (j < 6))
    def _():
        ogdn_ref[...] = y

    @pl.when(j >= 6)
    def _():
        ozg_ref[...] = y.astype(BF16)


def _mod_spec(rows_per_seq, tm, d, chunk, per_row):
    if per_row:
        return pl.BlockSpec((tm, d), lambda i, *_: (i, chunk))
    return pl.BlockSpec((None, 1, d), lambda i, *_: ((i * tm) // rows_per_seq, 0, chunk))


def _inproj(x, mod, rows_per_seq, per_row, g, w_main, b_main, w_small, b_small, width):
    r, d = x.shape
    n = w_main.shape[0]
    tn = width
    nj = n // tn
    n_heads = width // LANES
    tm = _tile(rows_per_seq if not per_row else r, 512)
    grid = (r // tm, nj)
    clip = lambda j, lo, hi: jnp.minimum(jnp.maximum(j - lo, 0), hi - lo)
    return pl.pallas_call(
        _inproj_kernel,
        grid=grid,
        in_specs=[pl.BlockSpec((tm, d), lambda i, j: (i, 0)),
                  pl.BlockSpec((1, d), lambda i, j: (0, 0)),
                  _mod_spec(rows_per_seq, tm, d, 0, per_row),
                  _mod_spec(rows_per_seq, tm, d, 1, per_row),
                  pl.BlockSpec((tn, d), lambda i, j: (j, 0)),
                  pl.BlockSpec((1, tn), lambda i, j: (0, j)),
                  pl.BlockSpec((LANES, d), lambda i, j: (0, 0)),
                  pl.BlockSpec((1, LANES), lambda i, j: (0, 0))],
        out_specs=[pl.BlockSpec((tm, tn), lambda i, j: (i, clip(j, 0, 2))),
                   pl.BlockSpec((tm, n_heads, LANES), lambda i, j: (i, 0, 0)),
                   pl.BlockSpec((tm, n_heads, LANES), lambda i, j: (i, 0, 0)),
                   pl.BlockSpec((tm, tn), lambda i, j: (i, clip(j, 3, 5))),
                   pl.BlockSpec((tm, tn), lambda i, j: (i, clip(j, 6, nj - 1))),
                   pl.BlockSpec((tm, LANES), lambda i, j: (i, 0))],
        out_shape=[jax.ShapeDtypeStruct((r, 3 * tn), BF16),
                   jax.ShapeDtypeStruct((r, n_heads, LANES), F32),
                   jax.ShapeDtypeStruct((r, n_heads, LANES), F32),
                   jax.ShapeDtypeStruct((r, 3 * tn), F32),
                   jax.ShapeDtypeStruct((r, (nj - 6) * tn), BF16),
                   jax.ShapeDtypeStruct((r, LANES), F32)],
        scratch_shapes=[pltpu.VMEM((tm, d), BF16)],
        compiler_params=_cparams(("arbitrary", "arbitrary")),
        name="inproj",
    )(x, g, mod, mod, w_main, b_main, w_small, b_small)


def _lambda_value(lq1, lk1, lq2, lk2, lam_init):
    s1 = jnp.sum(lq1 * lk1, axis=-1, keepdims=True)
    s2 = jnp.sum(lq2 * lk2, axis=-1, keepdims=True)
    return jnp.exp(s1) - jnp.exp(s2) + lam_init


def _diff_finish(acc, l, lam, subg, lam_init, half):
    o1 = acc[:half] / l[:half]
    o2 = acc[half:] / l[half:]
    a = o1 - lam * o2
    r = lax.rsqrt(jnp.mean(a * a, axis=-1, keepdims=True) + EPS)
    return ((a * r) * subg) * (1.0 - lam_init)


def _split_maps(q, scale):
    lane = lax.broadcasted_iota(jnp.int32, q.shape, 1)
    qs = q * scale
    zero = jnp.zeros_like(qs)
    half = q.shape[-1] // 2
    return jnp.concatenate([jnp.where(lane < half, qs, zero), jnp.where(lane >= half, qs, zero)], axis=0)


ONES_ROWS = 16


def _pattn_kernel(slopes_ref, lq1_ref, lk1_ref, lq2_ref, lk2_ref, subg_ref, q_ref, k_ref, v_ref,
                  o_ref, vt_ref, bias_ref, m_ref, acc_ref, sa_ref, sb_ref, *, tq, lam_init, scale):
    h = pl.program_id(1)
    qi = pl.program_id(2)
    slope = slopes_ref[h]
    seq = k_ref.shape[0]

    @pl.when(qi == 0)
    def _():
        for c in range(seq // tq):
            blk = v_ref[c * tq:(c + 1) * tq, :].astype(F32)
            vt_ref[0:LANES, c * tq:(c + 1) * tq] = blk.T.astype(BF16)
        vt_ref[LANES:, :] = jnp.ones((ONES_ROWS, seq), BF16)
        bias_ref[...] = slope * lax.broadcasted_iota(jnp.int32, bias_ref.shape, 0).astype(F32)

    qq = _split_maps(q_ref[...], scale)
    m_ref[...] = jnp.full(m_ref.shape, NEG, F32)
    acc_ref[...] = jnp.zeros(acc_ref.shape, F32)
    step = slope * tq

    def scores(c, dst_ref):
        start = pl.multiple_of(c * tq, tq)
        dst_ref[...] = _dot_nt(k_ref[pl.ds(start, tq), :], qq)

    def chunk(c, src_ref, masked):
        start = pl.multiple_of(c * tq, tq)
        st = src_ref[...] + bias_ref[...]
        if masked:
            kr = lax.broadcasted_iota(jnp.int32, st.shape, 0)
            qc = lax.broadcasted_iota(jnp.int32, st.shape, 1)
            qc = jnp.where(qc >= tq, qc - tq, qc)
            st = jnp.where(qc >= kr, st, NEG)
        m_prev = m_ref[...] - step
        m_new = jnp.maximum(m_prev, jnp.max(st, axis=0, keepdims=True))
        alpha = jnp.exp(m_prev - m_new)
        p = jnp.exp(st - m_new).astype(BF16)
        acc_ref[...] = alpha * acc_ref[...] + _dot(vt_ref[:, pl.ds(start, tq)], p)
        m_ref[...] = m_new

    scores(0, sa_ref)

    def pair(i, carry):
        c = 2 * i
        scores(c + 1, sb_ref)
        chunk(c, sa_ref, False)
        scores(c + 2, sa_ref)
        chunk(c + 1, sb_ref, False)
        return carry

    lax.fori_loop(0, qi // 2, pair, 0)

    @pl.when(qi % 2 == 1)
    def _():
        scores(qi, sb_ref)
        chunk(qi - 1, sa_ref, False)
        chunk(qi, sb_ref, True)

    @pl.when(qi % 2 == 0)
    def _():
        chunk(qi, sa_ref, True)

    lam = _lambda_value(lq1_ref[...], lk1_ref[...], lq2_ref[...], lk2_ref[...], lam_init)
    acc = acc_ref[...]
    on = acc[:LANES] / acc[LANES:LANES + 1]
    a = (on[:, :tq] - lam * on[:, tq:]).T
    r = lax.rsqrt(jnp.mean(a * a, axis=-1, keepdims=True) + EPS)
    o_ref[...] = (((a * r) * subg_ref[...]) * (1.0 - lam_init)).astype(o_ref.dtype)


def _prompt_attention(qkv, slopes, lam_vecs, subg, n_heads, lam_init):
    b, l, _ = qkv.shape
    tq = _tile(l, 256)
    hd = LANES
    vec = lambda n: pl.BlockSpec((1, n), lambda b_, h, i: (0, 0))
    kern = functools.partial(_pattn_kernel, tq=tq, lam_init=lam_init, scale=(hd // 2) ** -0.5)
    return pl.pallas_call(
        kern,
        grid=(b, n_heads, l // tq),
        in_specs=[pl.BlockSpec(memory_space=pltpu.SMEM),
                  vec(hd // 2), vec(hd // 2), vec(hd // 2), vec(hd // 2), vec(hd),
                  pl.BlockSpec((None, tq, hd), lambda b_, h, i: (b_, i, h)),
                  pl.BlockSpec((None, l, hd), lambda b_, h, i: (b_, 0, n_heads + h)),
                  pl.BlockSpec((None, l, hd), lambda b_, h, i: (b_, 0, 2 * n_heads + h))],
        out_specs=pl.BlockSpec((None, tq, hd), lambda b_, h, i: (b_, i, h)),
        out_shape=jax.ShapeDtypeStruct((b, l, n_heads * hd), BF16),
        scratch_shapes=[pltpu.VMEM((hd + ONES_ROWS, l), BF16), pltpu.VMEM((tq, 2 * tq), F32),
                        pltpu.VMEM((1, 2 * tq), F32), pltpu.VMEM((hd + ONES_ROWS, 2 * tq), F32),
                        pltpu.VMEM((tq, 2 * tq), F32), pltpu.VMEM((tq, 2 * tq), F32)],
        compiler_params=_cparams(("arbitrary", "arbitrary", "arbitrary")),
        name="prompt_attn",
    )(slopes, *lam_vecs, subg, qkv, qkv, qkv)


def _sattn_kernel(pt_ref, slope_ref, bias_ref, biasn_ref, lq1_ref, lk1_ref, lq2_ref, lk2_ref, subg_ref,
                  q_ref, kn_ref, vn_ref, *rest, n_pages_step, page, lam_init):
    k_refs = rest[:n_pages_step]
    v_refs = rest[n_pages_step:2 * n_pages_step]
    o_ref, m_ref, l_ref, acc_ref = rest[2 * n_pages_step:]
    g = pl.program_id(1)
    ng = pl.num_programs(1)

    @pl.when(g == 0)
    def _():
        m_ref[...] = jnp.full(m_ref.shape, NEG, F32)
        l_ref[...] = jnp.zeros(l_ref.shape, F32)
        acc_ref[...] = jnp.zeros(acc_ref.shape, F32)

    qq = q_ref[...]
    step = slope_ref[...] * (page * n_pages_step)

    def update(s, vs):
        m_prev = m_ref[...] - step
        m_new = jnp.maximum(m_prev, jnp.max(s, axis=-1, keepdims=True))
        alpha = jnp.exp(m_prev - m_new)
        p = jnp.exp(s - m_new)
        l_ref[...] = alpha * l_ref[...] + jnp.sum(p, axis=-1, keepdims=True)
        pb = p.astype(BF16)
        pv = None
        off = 0
        for vf in vs:
            part = _dot(pb[:, off:off + vf.shape[0]], vf)
            pv = part if pv is None else pv + part
            off += vf.shape[0]
        acc_ref[...] = alpha * acc_ref[...] + pv
        m_ref[...] = m_new

    s = jnp.concatenate([_dot_nt(qq, k_refs[i][...].reshape(-1, LANES).astype(BF16))
                         for i in range(n_pages_step)], axis=1) + bias_ref[...]
    update(s, [v_refs[i][...].reshape(-1, LANES).astype(BF16) for i in range(n_pages_step)])

    @pl.when(g == ng - 1)
    def _():
        update(_dot_nt(qq, kn_ref[...]) + biasn_ref[...], [vn_ref[...]])
        lam = _lambda_value(lq1_ref[...], lk1_ref[...], lq2_ref[...], lk2_ref[...], lam_init)
        half = acc_ref.shape[0] // 2
        o_ref[...] = _diff_finish(acc_ref[...], l_ref[...], lam, subg_ref[...], lam_init, half)


def _sample_attention(qrows, knew, vnew, cache_k, cache_v, layer, page_table, slopes, lam_vecs, subg, lam_init):
    db, nrow, hd = qrows.shape
    _, n_phys, page, n_heads, _ = cache_k.shape
    s_new = knew.shape[1] // n_heads
    n_pages = page_table.shape[1]
    pps = 8 if n_pages % 8 == 0 else 1
    r = jnp.arange(nrow)
    r_head = (r // s_new) % n_heads
    r_tok = r % s_new
    slope_rows = slopes[r_head][:, None]
    c = jnp.arange(pps * page * n_heads)
    bias = jnp.where(r_head[:, None] == (c % n_heads)[None, :], slope_rows * (c // n_heads)[None, :].astype(F32), NEG)
    cn = jnp.arange(s_new * n_heads)
    ok = (r_head[:, None] == (cn % n_heads)[None, :]) & ((cn // n_heads)[None, :] <= r_tok[:, None])
    biasn = jnp.where(ok, slope_rows * (cn // n_heads)[None, :].astype(F32), NEG)

    const = lambda shape: pl.BlockSpec(shape, lambda b, g, pt: (0,) * len(shape))
    page_spec = lambda i: pl.BlockSpec((None, None, page, n_heads, hd),
                                       lambda b, g, pt: (layer, pt[b, g * pps + i], 0, 0, 0))
    kern = functools.partial(_sattn_kernel, n_pages_step=pps, page=float(page), lam_init=lam_init)
    grid_spec = pltpu.PrefetchScalarGridSpec(
        num_scalar_prefetch=1,
        grid=(db, n_pages // pps),
        in_specs=[const((nrow, 1)), const((nrow, pps * page * n_heads)), const((nrow, s_new * n_heads)),
                  const((1, hd // 2)), const((1, hd // 2)), const((1, hd // 2)), const((1, hd // 2)), const((1, hd)),
                  pl.BlockSpec((None, nrow, hd), lambda b, g, pt: (b, 0, 0)),
                  pl.BlockSpec((None, s_new * n_heads, hd), lambda b, g, pt: (b, 0, 0)),
                  pl.BlockSpec((None, s_new * n_heads, hd), lambda b, g, pt: (b, 0, 0))]
                 + [page_spec(i) for i in range(pps)] + [page_spec(i) for i in range(pps)],
        out_specs=pl.BlockSpec((None, nrow // 2, hd), lambda b, g, pt: (b, 0, 0)),
        scratch_shapes=[pltpu.VMEM((nrow, 1), F32), pltpu.VMEM((nrow, 1), F32), pltpu.VMEM((nrow, hd), F32)],
    )
    return pl.pallas_call(
        kern,
        grid_spec=grid_spec,
        out_shape=jax.ShapeDtypeStruct((db, nrow // 2, hd), F32),
        compiler_params=_cparams(("arbitrary", "arbitrary")),
        name="sample_attn",
    )(page_table, slope_rows, bias, biasn, *lam_vecs, subg, qrows, knew, vnew,
      *([cache_k] * pps), *([cache_v] * pps))


def _gdn_prep_kernel(x_ref, sm_ref, st_ref, cw_ref, alog_ref, dtb_ref, q_ref, k_ref, v_ref, gate_ref, pad_ref,
                     *, tt, n_heads, valid_len, taps):
    t = pl.program_id(1)
    halo = 8
    w3 = n_heads * LANES

    @pl.when(t == 0)
    def _():
        pad_ref[0:halo, :] = jnp.zeros((halo, pad_ref.shape[1]), F32)
        pad_ref[halo - (taps - 1):halo, :] = st_ref[...]

    pad_ref[halo:halo + tt, :] = x_ref[...]
    y = pad_ref[halo:halo + tt, :] * cw_ref[taps - 1:taps, :]
    for j in range(taps - 1):
        off = halo - (taps - 1) + j
        y = y + pad_ref[off:off + tt, :] * cw_ref[j:j + 1, :]
    y = _silu(y)
    pad_ref[0:halo, :] = pad_ref[tt:tt + halo, :]

    def l2n(z):
        return z * lax.rsqrt(jnp.sum(z * z, axis=-1, keepdims=True) + EPS)

    for hh in range(n_heads):
        sl = slice(hh * LANES, (hh + 1) * LANES)
        q_ref[:, sl] = l2n(y[:, sl]) * (LANES ** -0.5)
        k_ref[:, sl] = l2n(y[:, w3 + hh * LANES:w3 + (hh + 1) * LANES])
    v_ref[...] = y[:, 2 * w3:]

    x = sm_ref[...]
    lane = lax.broadcasted_iota(jnp.int32, x.shape, 1)
    row = lax.broadcasted_iota(jnp.int32, x.shape, 0) + t * tt
    beta = jax.nn.sigmoid(x)
    z = x + dtb_ref[...]
    softplus = jnp.maximum(z, 0.0) + jnp.log(1.0 + jnp.exp(-jnp.abs(z)))
    gdec = -jnp.exp(alog_ref[...]) * softplus
    out = jnp.where(lane < n_heads, beta, gdec)
    gate_ref[...] = jnp.where((row < valid_len) & (lane < 2 * n_heads), out, 0.0)


def _gdn_prep(x, small, state, conv_w, a_log, dt_bias, n_heads, valid_len):
    b, l, ch = x.shape
    taps = conv_w.shape[0]
    tt = _tile(l, 256)
    w = n_heads * LANES
    lanes = jnp.arange(LANES)
    in_g = (lanes >= n_heads) & (lanes < 2 * n_heads)
    alog_v = jnp.where(in_g, a_log[jnp.clip(lanes - n_heads, 0, n_heads - 1)], 0.0).reshape(1, LANES)
    dtb_v = jnp.where(in_g, dt_bias[jnp.clip(lanes - n_heads, 0, n_heads - 1)], 0.0).reshape(1, LANES)
    kern = functools.partial(_gdn_prep_kernel, tt=tt, n_heads=n_heads, valid_len=valid_len, taps=taps)
    row_spec = lambda n: pl.BlockSpec((None, tt, n), lambda b_, t: (b_, t, 0))
    return pl.pallas_call(
        kern,
        grid=(b, l // tt),
        in_specs=[row_spec(ch), row_spec(LANES),
                  pl.BlockSpec((None, taps - 1, ch), lambda b_, t: (b_, 0, 0)),
                  pl.BlockSpec((taps, ch), lambda b_, t: (0, 0)),
                  pl.BlockSpec((1, LANES), lambda b_, t: (0, 0)),
                  pl.BlockSpec((1, LANES), lambda b_, t: (0, 0))],
        out_specs=[row_spec(w), row_spec(w), row_spec(w), row_spec(LANES)],
        out_shape=[jax.ShapeDtypeStruct((b, l, w), F32)] * 3 + [jax.ShapeDtypeStruct((b, l, LANES), F32)],
        scratch_shapes=[pltpu.VMEM((tt + 8, ch), F32)],
        compiler_params=_cparams(("arbitrary", "arbitrary")),
        name="gdn_prep",
    )(x, small, state, conv_w, alog_v, dtb_v)


def _gdn_kernel(q_ref, k_ref, v_ref, gate_ref, z_ref, gn_ref, s0_ref, o_ref, s_ref, *, chunk, n_chunks, n_heads):
    t = pl.program_id(1)

    @pl.when(t == 0)
    def _():
        s_ref[...] = s0_ref[...]

    c_ = chunk
    ri = lax.broadcasted_iota(jnp.int32, (c_, c_), 0)
    ci = lax.broadcasted_iota(jnp.int32, (c_, c_), 1)
    causal = ri >= ci
    strict = ri > ci
    tri = jnp.where(causal, 1.0, 0.0).astype(F32)
    eye = jnp.where(ri == ci, 1.0, 0.0).astype(F32)
    n_sq = max(int(math.log2(c_)) - 1, 0)
    gn = gn_ref[...]
    sel_r = lax.broadcasted_iota(jnp.int32, (8, LANES), 0)
    sel_c = lax.broadcasted_iota(jnp.int32, (8, LANES), 1)
    sel = jnp.where(sel_c == sel_r + n_heads, 1.0, 0.0).astype(F32)
    zpad = jnp.zeros((LANES - c_, LANES), F32)

    def one_chunk(c, carry):
        rows = pl.ds(c * c_ if n_chunks == 1 else pl.multiple_of(c * c_, c_), c_)
        gates = gate_ref[rows, :]
        gcum = jnp.dot(tri, gates, preferred_element_type=F32, precision=HI)
        gcum_t = _dot_nt(sel, gcum, precision=HI)
        hs = range(n_heads)
        sl = [slice(hh * LANES, (hh + 1) * LANES) for hh in hs]
        q = [q_ref[rows, sl[hh]] for hh in hs]
        k = [k_ref[rows, sl[hh]] for hh in hs]
        beta = [gates[:, hh:hh + 1] for hh in hs]
        gc = [gcum[:, n_heads + hh:n_heads + hh + 1] for hh in hs]
        glast = [gcum[c_ - 1:c_, n_heads + hh:n_heads + hh + 1] for hh in hs]
        decay = [jnp.exp(jnp.where(causal, gc[hh] - gcum_t[hh:hh + 1, :], NEG)) for hh in hs]
        kb = [k[hh].astype(BF16) for hh in hs]
        qk_kk = [_dot_nt(jnp.concatenate([kb[hh], q[hh].astype(BF16)], axis=0), kb[hh]) for hh in hs]
        a = [jnp.where(strict, beta[hh] * qk_kk[hh][:c_] * decay[hh], 0.0) for hh in hs]
        tinv = [eye - a[hh] for hh in hs]
        ap = a
        for _ in range(n_sq):
            apb = [ap[hh].astype(BF16) for hh in hs]
            ap = [_dot(apb[hh], apb[hh]) for hh in hs]
            tinv = [tinv[hh] + _dot(tinv[hh].astype(BF16), ap[hh].astype(BF16)) for hh in hs]
        eg = [jnp.exp(gc[hh]) for hh in hs]
        rhs = [jnp.concatenate([v_ref[rows, sl[hh]] * beta[hh], k[hh] * (beta[hh] * eg[hh])], axis=1).astype(BF16)
               for hh in hs]
        uw = [_dot(tinv[hh].astype(BF16), rhs[hh]) for hh in hs]
        s = [s_ref[hh] for hh in hs]
        wq = [_dot(jnp.concatenate([uw[hh][:, LANES:], q[hh] * eg[hh]], axis=0).astype(BF16), s[hh].astype(BF16))
              for hh in hs]
        u = [uw[hh][:, :LANES] - wq[hh][:c_] for hh in hs]
        ub = [u[hh].astype(BF16) for hh in hs]
        o = [wq[hh][c_:] + _dot((qk_kk[hh][c_:] * decay[hh]).astype(BF16), ub[hh]) for hh in hs]
        for hh in hs:
            kd = k[hh] * jnp.exp(glast[hh] - gc[hh])
            kd_t = jnp.concatenate([kd, zpad], axis=0).T.astype(BF16)
            u_pad = jnp.concatenate([u[hh], zpad], axis=0).astype(BF16)
            s_ref[hh] = s[hh] * jnp.exp(glast[hh]) + _dot(kd_t, u_pad)
        for hh in hs:
            r = lax.rsqrt(jnp.mean(o[hh] * o[hh], axis=-1, keepdims=True) + EPS)
            o_ref[rows, sl[hh]] = (((o[hh] * r) * gn) * _silu(z_ref[rows, sl[hh]].astype(F32))).astype(o_ref.dtype)
        return carry

    if n_chunks == 1:
        one_chunk(0, 0)
    else:
        lax.fori_loop(0, n_chunks, one_chunk, 0)


def _gdn(q, k, v, gates, z_arr, z_block, gn, s0, chunk, n_heads):
    b, l, w = q.shape
    tt = _tile(l, max(chunk, 256), chunk)
    kern = functools.partial(_gdn_kernel, chunk=chunk, n_chunks=tt // chunk, n_heads=n_heads)
    row_spec = lambda n: pl.BlockSpec((None, tt, n), lambda b_, t: (b_, t, 0))
    st_spec = pl.BlockSpec((None, n_heads, LANES, LANES), lambda b_, t: (b_, 0, 0, 0))
    return pl.pallas_call(
        kern,
        grid=(b, l // tt),
        in_specs=[row_spec(w), row_spec(w), row_spec(w), row_spec(LANES),
                  pl.BlockSpec((None, tt, w), lambda b_, t: (b_, t, z_block)),
                  pl.BlockSpec((1, LANES), lambda b_, t: (0, 0)),
                  st_spec],
        out_specs=[row_spec(w), st_spec],
        out_shape=[jax.ShapeDtypeStruct((b, l, w), BF16),
                   jax.ShapeDtypeStruct(s0.shape, F32)],
        compiler_params=_cparams(("arbitrary", "arbitrary")),
        name="gdn",
    )(q, k, v, gates, z_arr, gn, s0)


def _merge_kernel(a_ref, g_ref, ga_ref, gg_ref, wpa_ref, wpg_ref, wo_ref, x_ref, gate_ref, gn_ref, o_ref):
    pa = _dot(a_ref[...], wpa_ref[...])
    pg = _dot(g_ref[...], wpg_ref[...])
    m = jax.nn.sigmoid(ga_ref[...].astype(F32)) * pa + jax.nn.sigmoid(gg_ref[...].astype(F32)) * pg
    y = _dot(m.astype(BF16), wo_ref[...])
    r = lax.rsqrt(jnp.mean(y * y, axis=-1, keepdims=True) + EPS)
    o_ref[...] = x_ref[...] + gate_ref[...] * ((y * r) * gn_ref[...])


def _merge(a_out, g_out, zg, x, mod, rows_per_seq, per_row, w_pa, w_pg, w_o, gn):
    r, d = x.shape
    w = a_out.shape[1]
    tm = _tile(rows_per_seq if not per_row else r, 256)
    resident = lambda shape: pl.BlockSpec(shape, lambda i: (0, 0), pipeline_mode=pl.Buffered(1))
    return pl.pallas_call(
        _merge_kernel,
        grid=(r // tm,),
        in_specs=[pl.BlockSpec((tm, w), lambda i: (i, 0)),
                  pl.BlockSpec((tm, w), lambda i: (i, 0)),
                  pl.BlockSpec((tm, d), lambda i: (i, 0)),
                  pl.BlockSpec((tm, d), lambda i: (i, 1)),
                  resident((w, d)), resident((w, d)), resident((d, d)),
                  pl.BlockSpec((tm, d), lambda i: (i, 0)),
                  _mod_spec(rows_per_seq, tm, d, 2, per_row),
                  pl.BlockSpec((1, d), lambda i: (0, 0))],
        out_specs=pl.BlockSpec((tm, d), lambda i: (i, 0)),
        out_shape=jax.ShapeDtypeStruct((r, d), F32),
        compiler_params=_cparams(("arbitrary",)),
        name="merge",
    )(a_out, g_out, zg, zg, w_pa, w_pg, w_o, x, mod, gn)


def _ffn_up_kernel(x_ref, g_ref, sh_ref, sc_ref, wg_ref, wu_ref, o_ref, h_ref):
    @pl.when(pl.program_id(1) == 0)
    def _():
        h_ref[...] = _modulated(x_ref[...], g_ref[...], sh_ref[...], sc_ref[...]).astype(BF16)

    h = h_ref[...]
    gt = _dot(h, wg_ref[...])
    up = _dot(h, wu_ref[...])
    o_ref[...] = (_silu(gt) * up).astype(o_ref.dtype)


def _ffn_up(x, mod, rows_per_seq, per_row, g, w_up):
    r, d = x.shape
    f = w_up.shape[1] // 2
    tn = _tile(f, 512, LANES)
    nj = f // tn
    tm = _tile(rows_per_seq if not per_row else r, 512)
    return pl.pallas_call(
        _ffn_up_kernel,
        grid=(r // tm, nj),
        in_specs=[pl.BlockSpec((tm, d), lambda i, j: (i, 0)),
                  pl.BlockSpec((1, d), lambda i, j: (0, 0)),
                  _mod_spec(rows_per_seq, tm, d, 3, per_row),
                  _mod_spec(rows_per_seq, tm, d, 4, per_row),
                  pl.BlockSpec((d, tn), lambda i, j: (0, j)),
                  pl.BlockSpec((d, tn), lambda i, j: (0, nj + j))],
        out_specs=pl.BlockSpec((tm, tn), lambda i, j: (i, j)),
        out_shape=jax.ShapeDtypeStruct((r, f), BF16),
        scratch_shapes=[pltpu.VMEM((tm, d), BF16)],
        compiler_params=_cparams(("arbitrary", "arbitrary")),
        name="ffn_up",
    )(x, g, mod, mod, w_up, w_up)


def _ffn_down_kernel(a_ref, w_ref, x_ref, gate_ref, gn_ref, o_ref):
    y = _dot(a_ref[...], w_ref[...])
    r = lax.rsqrt(jnp.mean(y * y, axis=-1, keepdims=True) + EPS)
    o_ref[...] = x_ref[...] + gate_ref[...] * ((y * r) * gn_ref[...])


def _ffn_down(act, x, mod, rows_per_seq, per_row, w_down, gn):
    r, d = x.shape
    f = act.shape[1]
    tm = _tile(rows_per_seq if not per_row else r, 256)
    return pl.pallas_call(
        _ffn_down_kernel,
        grid=(r // tm,),
        in_specs=[pl.BlockSpec((tm, f), lambda i: (i, 0)),
                  pl.BlockSpec((f, d), lambda i: (0, 0), pipeline_mode=pl.Buffered(1)),
                  pl.BlockSpec((tm, d), lambda i: (i, 0)),
                  _mod_spec(rows_per_seq, tm, d, 5, per_row),
                  pl.BlockSpec((1, d), lambda i: (0, 0))],
        out_specs=pl.BlockSpec((tm, d), lambda i: (i, 0)),
        out_shape=jax.ShapeDtypeStruct((r, d), F32),
        compiler_params=_cparams(("arbitrary",)),
        name="ffn_down",
    )(act, w_down, x, mod, gn)


def _group(x, mod, per_row, attn_fn, conv_state, s0, chunk, valid_len, pw, n_heads):
    b, l, d = x.shape
    r = b * l
    w = n_heads * LANES
    xf = x.reshape(r, d)
    oattn, ok, ov, ogdn, ozg, osm = _inproj(xf, mod, l, per_row, pw["g_mix_pre"], pw["w_main"], pw["b_main"],
                                            pw["w_small"], pw["b_small"], w)
    a_out = attn_fn(oattn)
    q, k, v, gates = _gdn_prep(ogdn.reshape(b, l, 3 * w), osm.reshape(b, l, LANES), conv_state,
                               pw["conv_w"], pw["a_log"], pw["dt_bias"], n_heads, valid_len)
    g_out, s_new = _gdn(q, k, v, gates, ozg.reshape(b, l, -1), (2 * d) // w, pw["gdn_norm"], s0, chunk, n_heads)
    x1 = _merge(a_out, g_out.reshape(r, w), ozg, xf, mod, l, per_row, pw["w_pa"], pw["w_pg"], pw["w_o"],
                pw["g_mix_post"])
    act = _ffn_up(x1, mod, l, per_row, pw["g_ffn_pre"], pw["w_up"])
    x2 = _ffn_down(act, x1, mod, l, per_row, pw["w_down"], pw["g_ffn_post"])
    kv = (ok.reshape(b, l, n_heads, LANES), ov.reshape(b, l, n_heads, LANES))
    return x2.reshape(b, l, d), kv, ogdn, s_new


def kernel(x_prompt, x_sample, c_prompt, c_sample, cache_k, cache_v, page_table, state_ssm, state_conv, w_ada, b_ada, norm_mix_pre, norm_mix_post, norm_ffn_pre, norm_ffn_post, w_in, b_in, conv_w, lambda_q1, lambda_k1, lambda_q2, lambda_k2, attn_subln, a_log, dt_bias, gdn_norm, w_proj_attn, w_proj_gdn, w_out, w_ffn_up, w_ffn_down):
    bsz, seq, d = x_prompt.shape
    db, s_new, _ = x_sample.shape
    depth = w_in.shape[0]
    n_heads = cache_k.shape[3]
    hd = cache_k.shape[4]
    assert hd == LANES and state_ssm.shape[-1] == LANES and state_ssm.shape[-2] == LANES
    assert state_ssm.shape[2] == n_heads
    w = n_heads * hd
    taps = conv_w.shape[1]
    chunk_p = 64 if seq % 64 == 0 else seq
    slopes = 2.0 ** (-8.0 * jnp.arange(1, n_heads + 1, dtype=F32) / n_heads)
    assert (2 * d) % w == 0
    s_pad = -(-s_new // 16) * 16

    xp, xs = x_prompt, x_sample
    outs = [[] for _ in range(8)]
    for l in range(depth):
        lam_init = 0.8 - 0.6 * math.exp(-0.3 * l)
        lam_vecs = [t[l].reshape(1, -1) for t in (lambda_q1, lambda_k1, lambda_q2, lambda_k2)]
        subg = attn_subln[l].reshape(1, hd)

        c_all = jnp.concatenate([c_prompt, c_sample], axis=0)
        c_all = jnp.pad(c_all, ((0, (-c_all.shape[0]) % 8), (0, 0)))
        mod = _adaln(c_all, w_ada[l], b_ada[l])
        mod_p = mod[:bsz].reshape(bsz, 1, 6 * d)
        mod_s = jnp.repeat(mod[bsz:bsz + db], s_pad, axis=0)

        wt, bi = w_in[l].T, b_in[l]
        n6 = 6 * w
        n7 = 7 * w
        pw = dict(
            w_main=jnp.concatenate([wt[:n6], wt[n7 + 2 * n_heads:], wt[n6:n7]], axis=0).astype(BF16),
            b_main=jnp.concatenate([bi[:n6], bi[n7 + 2 * n_heads:], bi[n6:n7]]).reshape(1, -1),
            w_small=jnp.pad(wt[n7:n7 + 2 * n_heads], ((0, LANES - 2 * n_heads), (0, 0))).astype(BF16),
            b_small=jnp.pad(bi[n7:n7 + 2 * n_heads], (0, LANES - 2 * n_heads)).reshape(1, LANES),
            g_mix_pre=norm_mix_pre[l].reshape(1, d), g_mix_post=norm_mix_post[l].reshape(1, d),
            g_ffn_pre=norm_ffn_pre[l].reshape(1, d), g_ffn_post=norm_ffn_post[l].reshape(1, d),
            conv_w=conv_w[l], a_log=a_log[l], dt_bias=dt_bias[l], gdn_norm=gdn_norm[l].reshape(1, hd),
            w_pa=w_proj_attn[l].astype(BF16), w_pg=w_proj_gdn[l].astype(BF16), w_o=w_out[l].astype(BF16),
            w_up=w_ffn_up[l].astype(BF16), w_down=w_ffn_down[l].astype(BF16),
        )

        def attn_p(oattn):
            o = _prompt_attention(oattn.reshape(bsz, seq, 3 * w), slopes, lam_vecs, subg, n_heads, lam_init)
            return o.reshape(bsz * seq, w)

        conv0 = jnp.zeros((bsz, taps - 1, 3 * w), F32)
        s0 = jnp.zeros((bsz, n_heads, hd, hd), F32)
        xp, kv_p, raw_p, ssm_p = _group(xp, mod_p, False, attn_p, conv0, s0, chunk_p, seq, pw, n_heads)

        xs_pad = jnp.pad(xs, ((0, 0), (0, s_pad - s_new), (0, 0)))

        def attn_s(oattn):
            qkv = oattn.reshape(db, s_pad, 3, n_heads, hd)[:, :s_new]
            qh = qkv[:, :, 0].transpose(0, 2, 1, 3)
            qrows = jax.vmap(lambda t: _split_maps(t, (hd // 2) ** -0.5))(qh.reshape(db, n_heads * s_new, hd))
            knew = qkv[:, :, 1].reshape(db, s_new * n_heads, hd)
            vnew = qkv[:, :, 2].reshape(db, s_new * n_heads, hd)
            o = _sample_attention(qrows, knew, vnew, cache_k, cache_v, l, page_table, slopes, lam_vecs, subg,
                                  lam_init)
            o = o.reshape(db, n_heads, s_new, hd).transpose(0, 2, 1, 3).reshape(db, s_new, w)
            return jnp.pad(o, ((0, 0), (0, s_pad - s_new), (0, 0))).reshape(db * s_pad, w).astype(BF16)

        xs_full, kv_s, raw_s, ssm_s = _group(xs_pad, mod_s, True, attn_s, state_conv[l], state_ssm[l], s_pad,
                                             s_new, pw, n_heads)
        xs = xs_full[:, :s_new]

        raw_p = raw_p.reshape(bsz, seq, 3 * w)
        raw_s = jnp.concatenate([state_conv[l], raw_s.reshape(db, s_pad, 3 * w)[:, :s_new]], axis=1)
        for lst, val in zip(outs, (kv_p[0], kv_p[1], kv_s[0][:, :s_new], kv_s[1][:, :s_new], ssm_p, ssm_s,
                                   raw_p[:, seq - (taps - 1):], raw_s[:, s_new:])):
            lst.append(val)

    return (xp, xs) + tuple(jnp.stack(o) for o in outs)
```

```python
import functools
import math

import jax
import jax.numpy as jnp
from jax import lax
from jax.experimental import pallas as pl
from jax.experimental.pallas import tpu as pltpu

F32 = jnp.float32
BF16 = jnp.bfloat16
EPS = 1e-6
NEG = -1e30
LANES = 128
VMEM_LIMIT = 56 * 1024 * 1024
HI = lax.Precision.HIGHEST


def _cparams(sem):
    return pltpu.CompilerParams(dimension_semantics=sem, vmem_limit_bytes=VMEM_LIMIT)


def _tile(n, target, mult=8):
    if n <= target:
        return n
    for t in range(target, 0, -1):
        if n % t == 0 and t % mult == 0:
            return t
    return n


def _silu(x):
    return x * jax.nn.sigmoid(x)


def _dot(a, b):
    return jnp.dot(a, b, preferred_element_type=F32)


def _split(a):
    hi = a.astype(BF16)
    return hi, (a - hi.astype(F32)).astype(BF16)


def _dot3(a, b):
    ah, al = _split(a)
    bh, bl = _split(b)
    return _dot(ah, bh) + (_dot(ah, bl) + _dot(al, bh))


def _dot_nt(a, b, precision=None):
    return lax.dot_general(a, b, (((1,), (1,)), ((), ())), preferred_element_type=F32,
                           precision=precision)


def _dot_tn(a, b):
    return lax.dot_general(a, b, (((0,), (0,)), ((), ())), preferred_element_type=F32)


def _adaln_kernel(c_ref, w_ref, b_ref, o_ref):
    a = _silu(c_ref[...]).astype(BF16)
    o_ref[...] = _dot(a, w_ref[...].astype(BF16)) + b_ref[...]


def _adaln(c, w, b):
    m, d = c.shape
    n = w.shape[1]
    tn = _tile(n, 1024, LANES)
    return pl.pallas_call(
        _adaln_kernel,
        grid=(n // tn,),
        in_specs=[pl.BlockSpec((m, d), lambda j: (0, 0)),
                  pl.BlockSpec((d, tn), lambda j: (0, j)),
                  pl.BlockSpec((1, tn), lambda j: (0, j))],
        out_specs=pl.BlockSpec((m, tn), lambda j: (0, j)),
        out_shape=jax.ShapeDtypeStruct((m, n), F32),
        compiler_params=_cparams(("arbitrary",)),
        name="adaln",
    )(c, w, b.reshape(1, n))


def _modulated(x, g, shift, scale):
    r = lax.rsqrt(jnp.mean(x * x, axis=-1, keepdims=True) + EPS)
    return ((x * r) * g) * (1.0 + scale) + shift


def _inproj_kernel(x_ref, g_ref, sh_ref, sc_ref, wa_ref, wg_ref, b_ref, ws_ref, bs_ref,
                   oattn_ref, ok_ref, ov_ref, ogdn_ref, ozg_ref, osm_ref, h_ref):
    j = pl.program_id(1)
    nj = pl.num_programs(1)

    @pl.when(j == 0)
    def _():
        hb = _modulated(x_ref[...], g_ref[...], sh_ref[...], sc_ref[...]).astype(BF16)
        h_ref[...] = hb
        osm_ref[...] = _dot_nt(hb, ws_ref[...]) + bs_ref[...]

    def project(w_ref):
        return _dot_nt(h_ref[...], w_ref[...]) + b_ref[...]

    def heads_to_sublanes(y, o_ref):
        for hh in range(o_ref.shape[1]):
            o_ref[:, hh, :] = y[:, hh * LANES:(hh + 1) * LANES]

    @pl.when(j < 3)
    def _():
        y = project(wa_ref)
        oattn_ref[...] = y.astype(BF16)

        @pl.when(j == 1)
        def _():
            heads_to_sublanes(y, ok_ref)

        @pl.when(j == 2)
        def _():
            heads_to_sublanes(y, ov_ref)

    @pl.when((j >= 3) & (j < 6))
    def _():
        ogdn_ref[...] = project(wa_ref)

    @pl.when((j >= 6) & (j < nj - 1))
    def _():
        ozg_ref[...] = project(wg_ref).astype(BF16)

    @pl.when(j == nj - 1)
    def _():
        ozg_ref[...] = project(wa_ref).astype(BF16)


def _mod_spec(rows_per_seq, tm, d, chunk, per_row):
    if per_row:
        return pl.BlockSpec((tm, d), lambda i, *_: (i, chunk))
    return pl.BlockSpec((None, 1, d), lambda i, *_: ((i * tm) // rows_per_seq, 0, chunk))


def _inproj(x, mod, rows_per_seq, per_row, g, w_all, w_gates, b_main, w_small, b_small, width):
    r, d = x.shape
    tn = width
    n_gate = w_gates.shape[0] // tn
    nj = 7 + n_gate
    n_heads = width // LANES
    tm = _tile(rows_per_seq if not per_row else r, 512)
    grid = (r // tm, nj)
    clip = lambda j, lo, hi: jnp.minimum(jnp.maximum(j - lo, 0), hi - lo)
    return pl.pallas_call(
        _inproj_kernel,
        grid=grid,
        in_specs=[pl.BlockSpec((tm, d), lambda i, j: (i, 0)),
                  pl.BlockSpec((1, d), lambda i, j: (0, 0)),
                  _mod_spec(rows_per_seq, tm, d, 0, per_row),
                  _mod_spec(rows_per_seq, tm, d, 1, per_row),
                  pl.BlockSpec((tn, d), lambda i, j: (jnp.minimum(j, 6), 0)),
                  pl.BlockSpec((tn, d), lambda i, j: (clip(j, 6, 5 + n_gate), 0)),
                  pl.BlockSpec((1, tn), lambda i, j: (0, j)),
                  pl.BlockSpec((LANES, d), lambda i, j: (0, 0)),
                  pl.BlockSpec((1, LANES), lambda i, j: (0, 0))],
        out_specs=[pl.BlockSpec((tm, tn), lambda i, j: (i, clip(j, 0, 2))),
                   pl.BlockSpec((tm, n_heads, LANES), lambda i, j: (i, 0, 0)),
                   pl.BlockSpec((tm, n_heads, LANES), lambda i, j: (i, 0, 0)),
                   pl.BlockSpec((tm, tn), lambda i, j: (i, clip(j, 3, 5))),
                   pl.BlockSpec((tm, tn), lambda i, j: (i, clip(j, 6, nj - 1))),
                   pl.BlockSpec((tm, LANES), lambda i, j: (i, 0))],
        out_shape=[jax.ShapeDtypeStruct((r, 3 * tn), BF16),
                   jax.ShapeDtypeStruct((r, n_heads, LANES), F32),
                   jax.ShapeDtypeStruct((r, n_heads, LANES), F32),
                   jax.ShapeDtypeStruct((r, 3 * tn), F32),
                   jax.ShapeDtypeStruct((r, (nj - 6) * tn), BF16),
                   jax.ShapeDtypeStruct((r, LANES), F32)],
        scratch_shapes=[pltpu.VMEM((tm, d), BF16)],
        compiler_params=_cparams(("arbitrary", "arbitrary")),
        name="inproj",
    )(x, g, mod, mod, w_all, w_gates, b_main, w_small, b_small)


def _lambda_value(lq1, lk1, lq2, lk2, lam_init):
    s1 = jnp.sum(lq1 * lk1, axis=-1, keepdims=True)
    s2 = jnp.sum(lq2 * lk2, axis=-1, keepdims=True)
    return jnp.exp(s1) - jnp.exp(s2) + lam_init


def _diff_finish(acc, l, lam, subg, lam_init, half):
    o1 = acc[:half] / l[:half]
    o2 = acc[half:] / l[half:]
    a = o1 - lam * o2
    r = lax.rsqrt(jnp.mean(a * a, axis=-1, keepdims=True) + EPS)
    return ((a * r) * subg) * (1.0 - lam_init)


def _split_maps(q, scale):
    lane = lax.broadcasted_iota(jnp.int32, q.shape, 1)
    qs = q * scale
    zero = jnp.zeros_like(qs)
    half = q.shape[-1] // 2
    return jnp.concatenate([jnp.where(lane < half, qs, zero), jnp.where(lane >= half, qs, zero)], axis=0)


ONES_ROWS = 16


def _pattn_kernel(slopes_ref, lq1_ref, lk1_ref, lq2_ref, lk2_ref, subg_ref, q_ref, k_ref, v_ref,
                  o_ref, vt_ref, bias_ref, m_ref, acc_ref, sa_ref, sb_ref, *, tq, lam_init, scale):
    h = pl.program_id(1)
    qi = pl.program_id(2)
    slope = slopes_ref[h]
    seq = k_ref.shape[0]

    @pl.when(qi == 0)
    def _():
        for c in range(seq // tq):
            blk = v_ref[c * tq:(c + 1) * tq, :].astype(F32)
            vt_ref[0:LANES, c * tq:(c + 1) * tq] = blk.T.astype(BF16)
        vt_ref[LANES:, :] = jnp.ones((ONES_ROWS, seq), BF16)
        bias_ref[...] = slope * lax.broadcasted_iota(jnp.int32, bias_ref.shape, 0).astype(F32)

    qq = _split_maps(q_ref[...], scale)
    m_ref[...] = jnp.full(m_ref.shape, NEG, F32)
    acc_ref[...] = jnp.zeros(acc_ref.shape, F32)
    step = slope * tq

    def scores(c, dst_ref):
        start = pl.multiple_of(c * tq, tq)
        dst_ref[...] = _dot_nt(k_ref[pl.ds(start, tq), :], qq)

    def chunk(c, src_ref, masked):
        start = pl.multiple_of(c * tq, tq)
        st = src_ref[...] + bias_ref[...]
        if masked:
            kr = lax.broadcasted_iota(jnp.int32, st.shape, 0)
            qc = lax.broadcasted_iota(jnp.int32, st.shape, 1)
            qc = jnp.where(qc >= tq, qc - tq, qc)
            st = jnp.where(qc >= kr, st, NEG)
        m_prev = m_ref[...] - step
        m_new = jnp.maximum(m_prev, jnp.max(st, axis=0, keepdims=True))
        alpha = jnp.exp(m_prev - m_new)
        p = jnp.exp(st - m_new).astype(BF16)
        acc_ref[...] = alpha * acc_ref[...] + _dot(vt_ref[:, pl.ds(start, tq)], p)
        m_ref[...] = m_new

    scores(0, sa_ref)

    def pair(i, carry):
        c = 2 * i
        scores(c + 1, sb_ref)
        chunk(c, sa_ref, False)
        scores(c + 2, sa_ref)
        chunk(c + 1, sb_ref, False)
        return carry

    lax.fori_loop(0, qi // 2, pair, 0)

    @pl.when(qi % 2 == 1)
    def _():
        scores(qi, sb_ref)
        chunk(qi - 1, sa_ref, False)
        chunk(qi, sb_ref, True)

    @pl.when(qi % 2 == 0)
    def _():
        chunk(qi, sa_ref, True)

    lam = _lambda_value(lq1_ref[...], lk1_ref[...], lq2_ref[...], lk2_ref[...], lam_init)
    acc = acc_ref[...]
    on = acc[:LANES] / acc[LANES:LANES + 1]
    a = (on[:, :tq] - lam * on[:, tq:]).T
    r = lax.rsqrt(jnp.mean(a * a, axis=-1, keepdims=True) + EPS)
    o_ref[...] = (((a * r) * subg_ref[...]) * (1.0 - lam_init)).astype(o_ref.dtype)


def _prompt_attention(qkv, slopes, lam_vecs, subg, n_heads, lam_init):
    b, l, _ = qkv.shape
    tq = _tile(l, 256)
    hd = LANES
    vec = lambda n: pl.BlockSpec((1, n), lambda b_, h, i: (0, 0))
    kern = functools.partial(_pattn_kernel, tq=tq, lam_init=lam_init, scale=(hd // 2) ** -0.5)
    return pl.pallas_call(
        kern,
        grid=(b, n_heads, l // tq),
        in_specs=[pl.BlockSpec(memory_space=pltpu.SMEM),
                  vec(hd // 2), vec(hd // 2), vec(hd // 2), vec(hd // 2), vec(hd),
                  pl.BlockSpec((None, tq, hd), lambda b_, h, i: (b_, i, h)),
                  pl.BlockSpec((None, l, hd), lambda b_, h, i: (b_, 0, n_heads + h)),
                  pl.BlockSpec((None, l, hd), lambda b_, h, i: (b_, 0, 2 * n_heads + h))],
        out_specs=pl.BlockSpec((None, tq, hd), lambda b_, h, i: (b_, i, h)),
        out_shape=jax.ShapeDtypeStruct((b, l, n_heads * hd), BF16),
        scratch_shapes=[pltpu.VMEM((hd + ONES_ROWS, l), BF16), pltpu.VMEM((tq, 2 * tq), F32),
                        pltpu.VMEM((1, 2 * tq), F32), pltpu.VMEM((hd + ONES_ROWS, 2 * tq), F32),
                        pltpu.VMEM((tq, 2 * tq), F32), pltpu.VMEM((tq, 2 * tq), F32)],
        compiler_params=_cparams(("arbitrary", "arbitrary", "arbitrary")),
        name="prompt_attn",
    )(slopes, *lam_vecs, subg, qkv, qkv, qkv)


def _sattn_kernel(pt_ref, slope_ref, bias_ref, biasn_ref, lq1_ref, lk1_ref, lq2_ref, lk2_ref, subg_ref,
                  q_ref, kn_ref, vn_ref, *rest, n_pages_step, page, lam_init):
    k_refs = rest[:n_pages_step]
    v_refs = rest[n_pages_step:2 * n_pages_step]
    o_ref, m_ref, l_ref, acc_ref = rest[2 * n_pages_step:]
    g = pl.program_id(1)
    ng = pl.num_programs(1)

    @pl.when(g == 0)
    def _():
        m_ref[...] = jnp.full(m_ref.shape, NEG, F32)
        l_ref[...] = jnp.zeros(l_ref.shape, F32)
        acc_ref[...] = jnp.zeros(acc_ref.shape, F32)

    qq = q_ref[...]
    step = slope_ref[...] * (page * n_pages_step)

    def update(s, vs):
        m_prev = m_ref[...] - step
        m_new = jnp.maximum(m_prev, jnp.max(s, axis=-1, keepdims=True))
        alpha = jnp.exp(m_prev - m_new)
        p = jnp.exp(s - m_new)
        l_ref[...] = alpha * l_ref[...] + jnp.sum(p, axis=-1, keepdims=True)
        pb = p.astype(BF16)
        pv = None
        off = 0
        for vf in vs:
            part = _dot(pb[:, off:off + vf.shape[0]], vf)
            pv = part if pv is None else pv + part
            off += vf.shape[0]
        acc_ref[...] = alpha * acc_ref[...] + pv
        m_ref[...] = m_new

    s = jnp.concatenate([_dot_nt(qq, k_refs[i][...].reshape(-1, LANES).astype(BF16))
                         for i in range(n_pages_step)], axis=1) + bias_ref[...]
    update(s, [v_refs[i][...].reshape(-1, LANES).astype(BF16) for i in range(n_pages_step)])

    @pl.when(g == ng - 1)
    def _():
        update(_dot_nt(qq, kn_ref[...]) + biasn_ref[...], [vn_ref[...]])
        lam = _lambda_value(lq1_ref[...], lk1_ref[...], lq2_ref[...], lk2_ref[...], lam_init)
        half = acc_ref.shape[0] // 2
        o_ref[...] = _diff_finish(acc_ref[...], l_ref[...], lam, subg_ref[...], lam_init, half)


def _sample_attention(qrows, knew, vnew, cache_k, cache_v, layer, page_table, slopes, lam_vecs, subg, lam_init):
    db, nrow, hd = qrows.shape
    _, n_phys, page, n_heads, _ = cache_k.shape
    s_new = knew.shape[1] // n_heads
    n_pages = page_table.shape[1]
    pps = 8 if n_pages % 8 == 0 else 1
    r = jnp.arange(nrow)
    r_head = (r // s_new) % n_heads
    r_tok = r % s_new
    slope_rows = slopes[r_head][:, None]
    c = jnp.arange(pps * page * n_heads)
    bias = jnp.where(r_head[:, None] == (c % n_heads)[None, :], slope_rows * (c // n_heads)[None, :].astype(F32), NEG)
    cn = jnp.arange(s_new * n_heads)
    ok = (r_head[:, None] == (cn % n_heads)[None, :]) & ((cn // n_heads)[None, :] <= r_tok[:, None])
    biasn = jnp.where(ok, slope_rows * (cn // n_heads)[None, :].astype(F32), NEG)

    const = lambda shape: pl.BlockSpec(shape, lambda b, g, pt: (0,) * len(shape))
    page_spec = lambda i: pl.BlockSpec((None, None, page, n_heads, hd),
                                       lambda b, g, pt: (layer, pt[b, g * pps + i], 0, 0, 0))
    kern = functools.partial(_sattn_kernel, n_pages_step=pps, page=float(page), lam_init=lam_init)
    grid_spec = pltpu.PrefetchScalarGridSpec(
        num_scalar_prefetch=1,
        grid=(db, n_pages // pps),
        in_specs=[const((nrow, 1)), const((nrow, pps * page * n_heads)), const((nrow, s_new * n_heads)),
                  const((1, hd // 2)), const((1, hd // 2)), const((1, hd // 2)), const((1, hd // 2)), const((1, hd)),
                  pl.BlockSpec((None, nrow, hd), lambda b, g, pt: (b, 0, 0)),
                  pl.BlockSpec((None, s_new * n_heads, hd), lambda b, g, pt: (b, 0, 0)),
                  pl.BlockSpec((None, s_new * n_heads, hd), lambda b, g, pt: (b, 0, 0))]
                 + [page_spec(i) for i in range(pps)] + [page_spec(i) for i in range(pps)],
        out_specs=pl.BlockSpec((None, nrow // 2, hd), lambda b, g, pt: (b, 0, 0)),
        scratch_shapes=[pltpu.VMEM((nrow, 1), F32), pltpu.VMEM((nrow, 1), F32), pltpu.VMEM((nrow, hd), F32)],
    )
    return pl.pallas_call(
        kern,
        grid_spec=grid_spec,
        out_shape=jax.ShapeDtypeStruct((db, nrow // 2, hd), F32),
        compiler_params=_cparams(("arbitrary", "arbitrary")),
        name="sample_attn",
    )(page_table, slope_rows, bias, biasn, *lam_vecs, subg, qrows, knew, vnew,
      *([cache_k] * pps), *([cache_v] * pps))


def _gdn_prep_kernel(x_ref, sm_ref, st_ref, cw_ref, alog_ref, dtb_ref, q_ref, k_ref, v_ref, gate_ref, pad_ref,
                     *, tt, n_heads, valid_len, taps):
    t = pl.program_id(1)
    halo = 8
    w3 = n_heads * LANES

    @pl.when(t == 0)
    def _():
        pad_ref[0:halo, :] = jnp.zeros((halo, pad_ref.shape[1]), F32)
        pad_ref[halo - (taps - 1):halo, :] = st_ref[...]

    pad_ref[halo:halo + tt, :] = x_ref[...]
    y = pad_ref[halo:halo + tt, :] * cw_ref[taps - 1:taps, :]
    for j in range(taps - 1):
        off = halo - (taps - 1) + j
        y = y + pad_ref[off:off + tt, :] * cw_ref[j:j + 1, :]
    y = _silu(y)
    pad_ref[0:halo, :] = pad_ref[tt:tt + halo, :]

    def l2n(z):
        return z * lax.rsqrt(jnp.sum(z * z, axis=-1, keepdims=True) + EPS)

    for hh in range(n_heads):
        sl = slice(hh * LANES, (hh + 1) * LANES)
        q_ref[:, sl] = l2n(y[:, sl]) * (LANES ** -0.5)
        k_ref[:, sl] = l2n(y[:, w3 + hh * LANES:w3 + (hh + 1) * LANES])
    v_ref[...] = y[:, 2 * w3:]

    x = sm_ref[...]
    lane = lax.broadcasted_iota(jnp.int32, x.shape, 1)
    row = lax.broadcasted_iota(jnp.int32, x.shape, 0) + t * tt
    beta = jax.nn.sigmoid(x)
    z = x + dtb_ref[...]
    softplus = jnp.maximum(z, 0.0) + jnp.log(1.0 + jnp.exp(-jnp.abs(z)))
    gdec = -jnp.exp(alog_ref[...]) * softplus
    out = jnp.where(lane < n_heads, beta, gdec)
    gate_ref[...] = jnp.where((row < valid_len) & (lane < 2 * n_heads), out, 0.0)


def _gdn_prep(x, small, state, conv_w, a_log, dt_bias, n_heads, valid_len):
    b, l, ch = x.shape
    taps = conv_w.shape[0]
    tt = _tile(l, 256)
    w = n_heads * LANES
    lanes = jnp.arange(LANES)
    in_g = (lanes >= n_heads) & (lanes < 2 * n_heads)
    alog_v = jnp.where(in_g, a_log[jnp.clip(lanes - n_heads, 0, n_heads - 1)], 0.0).reshape(1, LANES)
    dtb_v = jnp.where(in_g, dt_bias[jnp.clip(lanes - n_heads, 0, n_heads - 1)], 0.0).reshape(1, LANES)
    kern = functools.partial(_gdn_prep_kernel, tt=tt, n_heads=n_heads, valid_len=valid_len, taps=taps)
    row_spec = lambda n: pl.BlockSpec((None, tt, n), lambda b_, t: (b_, t, 0))
    return pl.pallas_call(
        kern,
        grid=(b, l // tt),
        in_specs=[row_spec(ch), row_spec(LANES),
                  pl.BlockSpec((None, taps - 1, ch), lambda b_, t: (b_, 0, 0)),
                  pl.BlockSpec((taps, ch), lambda b_, t: (0, 0)),
                  pl.BlockSpec((1, LANES), lambda b_, t: (0, 0)),
                  pl.BlockSpec((1, LANES), lambda b_, t: (0, 0))],
        out_specs=[row_spec(w), row_spec(w), row_spec(w), row_spec(LANES)],
        out_shape=[jax.ShapeDtypeStruct((b, l, w), F32)] * 3 + [jax.ShapeDtypeStruct((b, l, LANES), F32)],
        scratch_shapes=[pltpu.VMEM((tt + 8, ch), F32)],
        compiler_params=_cparams(("arbitrary", "arbitrary")),
        name="gdn_prep",
    )(x, small, state, conv_w, alog_v, dtb_v)


def _gdn_kernel(q_ref, k_ref, v_ref, gate_ref, z_ref, gn_ref, s0_ref, o_ref, s_ref, *, chunk, n_chunks, n_heads):
    t = pl.program_id(1)

    @pl.when(t == 0)
    def _():
        s_ref[...] = s0_ref[...]

    c_ = chunk
    ri = lax.broadcasted_iota(jnp.int32, (c_, c_), 0)
    ci = lax.broadcasted_iota(jnp.int32, (c_, c_), 1)
    causal = ri >= ci
    strict = ri > ci
    tri = jnp.where(causal, 1.0, 0.0).astype(F32)
    eye = jnp.where(ri == ci, 1.0, 0.0).astype(F32)
    n_sq = max(int(math.log2(c_)) - 1, 0)
    gn = gn_ref[...]
    sel_r = lax.broadcasted_iota(jnp.int32, (8, LANES), 0)
    sel_c = lax.broadcasted_iota(jnp.int32, (8, LANES), 1)
    sel = jnp.where(sel_c == sel_r + n_heads, 1.0, 0.0).astype(F32)
    zpad = jnp.zeros((LANES - c_, LANES), F32)

    def one_chunk(c, carry):
        rows = pl.ds(c * c_ if n_chunks == 1 else pl.multiple_of(c * c_, c_), c_)
        gates = gate_ref[rows, :]
        gcum = jnp.dot(tri, gates, preferred_element_type=F32, precision=HI)
        gcum_t = _dot_nt(sel, gcum, precision=HI)
        hs = range(n_heads)
        sl = [slice(hh * LANES, (hh + 1) * LANES) for hh in hs]
        q = [q_ref[rows, sl[hh]] for hh in hs]
        k = [k_ref[rows, sl[hh]] for hh in hs]
        beta = [gates[:, hh:hh + 1] for hh in hs]
        gc = [gcum[:, n_heads + hh:n_heads + hh + 1] for hh in hs]
        glast = [gcum[c_ - 1:c_, n_heads + hh:n_heads + hh + 1] for hh in hs]
        decay = [jnp.exp(jnp.where(causal, gc[hh] - gcum_t[hh:hh + 1, :], NEG)) for hh in hs]
        kb = [k[hh].astype(BF16) for hh in hs]
        qk_kk = [_dot_nt(jnp.concatenate([kb[hh], q[hh].astype(BF16)], axis=0), kb[hh]) for hh in hs]
        a = [jnp.where(strict, beta[hh] * qk_kk[hh][:c_] * decay[hh], 0.0) for hh in hs]
        tinv = [eye - a[hh] for hh in hs]
        ap = a
        for _ in range(n_sq):
            ap = [_dot3(ap[hh], ap[hh]) for hh in hs]
            tinv = [tinv[hh] + _dot3(tinv[hh], ap[hh]) for hh in hs]
        eg = [jnp.exp(gc[hh]) for hh in hs]
        rhs = [jnp.concatenate([v_ref[rows, sl[hh]] * beta[hh], k[hh] * (beta[hh] * eg[hh])], axis=1).astype(BF16)
               for hh in hs]
        uw = [_dot(tinv[hh].astype(BF16), rhs[hh]) for hh in hs]
        s = [s_ref[hh] for hh in hs]
        wq = [_dot(jnp.concatenate([uw[hh][:, LANES:], q[hh] * eg[hh]], axis=0).astype(BF16), s[hh].astype(BF16))
              for hh in hs]
        u = [uw[hh][:, :LANES] - wq[hh][:c_] for hh in hs]
        ub = [u[hh].astype(BF16) for hh in hs]
        o = [wq[hh][c_:] + _dot((qk_kk[hh][c_:] * decay[hh]).astype(BF16), ub[hh]) for hh in hs]
        for hh in hs:
            kd = k[hh] * jnp.exp(glast[hh] - gc[hh])
            kd_t = jnp.concatenate([kd, zpad], axis=0).T.astype(BF16)
            u_pad = jnp.concatenate([u[hh], zpad], axis=0).astype(BF16)
            s_ref[hh] = s[hh] * jnp.exp(glast[hh]) + _dot(kd_t, u_pad)
        for hh in hs:
            r = lax.rsqrt(jnp.mean(o[hh] * o[hh], axis=-1, keepdims=True) + EPS)
            o_ref[rows, sl[hh]] = (((o[hh] * r) * gn) * _silu(z_ref[rows, sl[hh]].astype(F32))).astype(o_ref.dtype)
        return carry

    if n_chunks == 1:
        one_chunk(0, 0)
    else:
        lax.fori_loop(0, n_chunks, one_chunk, 0)


def _gdn(q, k, v, gates, z_arr, z_block, gn, s0, chunk, n_heads):
    b, l, w = q.shape
    tt = _tile(l, max(chunk, 256), chunk)
    kern = functools.partial(_gdn_kernel, chunk=chunk, n_chunks=tt // chunk, n_heads=n_heads)
    row_spec = lambda n: pl.BlockSpec((None, tt, n), lambda b_, t: (b_, t, 0))
    st_spec = pl.BlockSpec((None, n_heads, LANES, LANES), lambda b_, t: (b_, 0, 0, 0))
    return pl.pallas_call(
        kern,
        grid=(b, l // tt),
        in_specs=[row_spec(w), row_spec(w), row_spec(w), row_spec(LANES),
                  pl.BlockSpec((None, tt, w), lambda b_, t: (b_, t, z_block)),
                  pl.BlockSpec((1, LANES), lambda b_, t: (0, 0)),
                  st_spec],
        out_specs=[row_spec(w), st_spec],
        out_shape=[jax.ShapeDtypeStruct((b, l, w), BF16),
                   jax.ShapeDtypeStruct(s0.shape, F32)],
        compiler_params=_cparams(("arbitrary", "arbitrary")),
        name="gdn",
    )(q, k, v, gates, z_arr, gn, s0)


def _merge_kernel(a_ref, g_ref, ga_ref, gg_ref, wpa_ref, wpg_ref, wo_ref, x_ref, gate_ref, gn_ref, o_ref):
    pa = _dot(a_ref[...], wpa_ref[...])
    pg = _dot(g_ref[...], wpg_ref[...])
    m = jax.nn.sigmoid(ga_ref[...].astype(F32)) * pa + jax.nn.sigmoid(gg_ref[...].astype(F32)) * pg
    y = _dot(m.astype(BF16), wo_ref[...])
    r = lax.rsqrt(jnp.mean(y * y, axis=-1, keepdims=True) + EPS)
    o_ref[...] = x_ref[...] + gate_ref[...] * ((y * r) * gn_ref[...])


def _merge(a_out, g_out, zg, x, mod, rows_per_seq, per_row, w_pa, w_pg, w_o, gn):
    r, d = x.shape
    w = a_out.shape[1]
    tm = _tile(rows_per_seq if not per_row else r, 256)
    resident = lambda shape: pl.BlockSpec(shape, lambda i: (0, 0), pipeline_mode=pl.Buffered(1))
    return pl.pallas_call(
        _merge_kernel,
        grid=(r // tm,),
        in_specs=[pl.BlockSpec((tm, w), lambda i: (i, 0)),
                  pl.BlockSpec((tm, w), lambda i: (i, 0)),
                  pl.BlockSpec((tm, d), lambda i: (i, 0)),
                  pl.BlockSpec((tm, d), lambda i: (i, 1)),
                  resident((w, d)), resident((w, d)), resident((d, d)),
                  pl.BlockSpec((tm, d), lambda i: (i, 0)),
                  _mod_spec(rows_per_seq, tm, d, 2, per_row),
                  pl.BlockSpec((1, d), lambda i: (0, 0))],
        out_specs=pl.BlockSpec((tm, d), lambda i: (i, 0)),
        out_shape=jax.ShapeDtypeStruct((r, d), F32),
        compiler_params=_cparams(("arbitrary",)),
        name="merge",
    )(a_out, g_out, zg, zg, w_pa, w_pg, w_o, x, mod, gn)


def _ffn_up_kernel(x_ref, g_ref, sh_ref, sc_ref, wg_ref, wu_ref, o_ref, h_ref):
    @pl.when(pl.program_id(1) == 0)
    def _():
        h_ref[...] = _modulated(x_ref[...], g_ref[...], sh_ref[...], sc_ref[...]).astype(BF16)

    h = h_ref[...]
    gt = _dot(h, wg_ref[...])
    up = _dot(h, wu_ref[...])
    o_ref[...] = (_silu(gt) * up).astype(o_ref.dtype)


def _ffn_up(x, mod, rows_per_seq, per_row, g, w_up):
    r, d = x.shape
    f = w_up.shape[1] // 2
    tn = _tile(f, 512, LANES)
    nj = f // tn
    tm = _tile(rows_per_seq if not per_row else r, 512)
    return pl.pallas_call(
        _ffn_up_kernel,
        grid=(r // tm, nj),
        in_specs=[pl.BlockSpec((tm, d), lambda i, j: (i, 0)),
                  pl.BlockSpec((1, d), lambda i, j: (0, 0)),
                  _mod_spec(rows_per_seq, tm, d, 3, per_row),
                  _mod_spec(rows_per_seq, tm, d, 4, per_row),
                  pl.BlockSpec((d, tn), lambda i, j: (0, j)),
                  pl.BlockSpec((d, tn), lambda i, j: (0, nj + j))],
        out_specs=pl.BlockSpec((tm, tn), lambda i, j: (i, j)),
        out_shape=jax.ShapeDtypeStruct((r, f), BF16),
        scratch_shapes=[pltpu.VMEM((tm, d), BF16)],
        compiler_params=_cparams(("arbitrary", "arbitrary")),
        name="ffn_up",
    )(x, g, mod, mod, w_up, w_up)


def _ffn_down_kernel(a_ref, w_ref, x_ref, gate_ref, gn_ref, o_ref):
    y = _dot(a_ref[...], w_ref[...])
    r = lax.rsqrt(jnp.mean(y * y, axis=-1, keepdims=True) + EPS)
    o_ref[...] = x_ref[...] + gate_ref[...] * ((y * r) * gn_ref[...])


def _ffn_down(act, x, mod, rows_per_seq, per_row, w_down, gn):
    r, d = x.shape
    f = act.shape[1]
    tm = _tile(rows_per_seq if not per_row else r, 256)
    return pl.pallas_call(
        _ffn_down_kernel,
        grid=(r // tm,),
        in_specs=[pl.BlockSpec((tm, f), lambda i: (i, 0)),
                  pl.BlockSpec((f, d), lambda i: (0, 0), pipeline_mode=pl.Buffered(1)),
                  pl.BlockSpec((tm, d), lambda i: (i, 0)),
                  _mod_spec(rows_per_seq, tm, d, 5, per_row),
                  pl.BlockSpec((1, d), lambda i: (0, 0))],
        out_specs=pl.BlockSpec((tm, d), lambda i: (i, 0)),
        out_shape=jax.ShapeDtypeStruct((r, d), F32),
        compiler_params=_cparams(("arbitrary",)),
        name="ffn_down",
    )(act, w_down, x, mod, gn)


def _group(x, mod, per_row, attn_fn, conv_state, s0, chunk, valid_len, pw, n_heads):
    b, l, d = x.shape
    r = b * l
    w = n_heads * LANES
    xf = x.reshape(r, d)
    oattn, ok, ov, ogdn, ozg, osm = _inproj(xf, mod, l, per_row, pw["g_mix_pre"], pw["w_all"], pw["w_gates"],
                                            pw["b_main"], pw["w_small"], pw["b_small"], w)
    a_out = attn_fn(oattn)
    q, k, v, gates = _gdn_prep(ogdn.reshape(b, l, 3 * w), osm.reshape(b, l, LANES), conv_state,
                               pw["conv_w"], pw["a_log"], pw["dt_bias"], n_heads, valid_len)
    g_out, s_new = _gdn(q, k, v, gates, ozg.reshape(b, l, -1), (2 * d) // w, pw["gdn_norm"], s0, chunk, n_heads)
    x1 = _merge(a_out, g_out.reshape(r, w), ozg, xf, mod, l, per_row, pw["w_pa"], pw["w_pg"], pw["w_o"],
                pw["g_mix_post"])
    act = _ffn_up(x1, mod, l, per_row, pw["g_ffn_pre"], pw["w_up"])
    x2 = _ffn_down(act, x1, mod, l, per_row, pw["w_down"], pw["g_ffn_post"])
    kv = (ok.reshape(b, l, n_heads, LANES), ov.reshape(b, l, n_heads, LANES))
    return x2.reshape(b, l, d), kv, ogdn, s_new


def kernel(x_prompt, x_sample, c_prompt, c_sample, cache_k, cache_v, page_table, state_ssm, state_conv, w_ada, b_ada, norm_mix_pre, norm_mix_post, norm_ffn_pre, norm_ffn_post, w_in, b_in, conv_w, lambda_q1, lambda_k1, lambda_q2, lambda_k2, attn_subln, a_log, dt_bias, gdn_norm, w_proj_attn, w_proj_gdn, w_out, w_ffn_up, w_ffn_down):
    bsz, seq, d = x_prompt.shape
    db, s_new, _ = x_sample.shape
    depth = w_in.shape[0]
    n_heads = cache_k.shape[3]
    hd = cache_k.shape[4]
    assert hd == LANES and state_ssm.shape[-1] == LANES and state_ssm.shape[-2] == LANES
    assert state_ssm.shape[2] == n_heads
    w = n_heads * hd
    taps = conv_w.shape[1]
    chunk_p = 64 if seq % 64 == 0 else seq
    slopes = 2.0 ** (-8.0 * jnp.arange(1, n_heads + 1, dtype=F32) / n_heads)
    assert (2 * d) % w == 0
    s_pad = -(-s_new // 8) * 8

    xp, xs = x_prompt, x_sample
    outs = [[] for _ in range(8)]
    for l in range(depth):
        lam_init = 0.8 - 0.6 * math.exp(-0.3 * l)
        lam_vecs = [t[l].reshape(1, -1) for t in (lambda_q1, lambda_k1, lambda_q2, lambda_k2)]
        subg = attn_subln[l].reshape(1, hd)

        c_all = jnp.concatenate([c_prompt, c_sample], axis=0)
        c_all = jnp.pad(c_all, ((0, (-c_all.shape[0]) % 8), (0, 0)))
        mod = _adaln(c_all, w_ada[l], b_ada[l])
        mod_p = mod[:bsz].reshape(bsz, 1, 6 * d)
        mod_s = jnp.repeat(mod[bsz:bsz + db], s_pad, axis=0)

        wt, bi = w_in[l].T, b_in[l]
        n6 = 6 * w
        n7 = 7 * w
        w_all = wt.astype(BF16)
        pw = dict(
            w_all=w_all, w_gates=w_all[n7 + 2 * n_heads:],
            b_main=jnp.concatenate([bi[:n6], bi[n7 + 2 * n_heads:], bi[n6:n7]]).reshape(1, -1),
            w_small=jnp.pad(w_all[n7:n7 + 2 * n_heads], ((0, LANES - 2 * n_heads), (0, 0))),
            b_small=jnp.pad(bi[n7:n7 + 2 * n_heads], (0, LANES - 2 * n_heads)).reshape(1, LANES),
            g_mix_pre=norm_mix_pre[l].reshape(1, d), g_mix_post=norm_mix_post[l].reshape(1, d),
            g_ffn_pre=norm_ffn_pre[l].reshape(1, d), g_ffn_post=norm_ffn_post[l].reshape(1, d),
            conv_w=conv_w[l], a_log=a_log[l], dt_bias=dt_bias[l], gdn_norm=gdn_norm[l].reshape(1, hd),
            w_pa=w_proj_attn[l].astype(BF16), w_pg=w_proj_gdn[l].astype(BF16), w_o=w_out[l].astype(BF16),
            w_up=w_ffn_up[l].astype(BF16), w_down=w_ffn_down[l].astype(BF16),
        )

        def attn_p(oattn):
            o = _prompt_attention(oattn.reshape(bsz, seq, 3 * w), slopes, lam_vecs, subg, n_heads, lam_init)
            return o.reshape(bsz * seq, w)

        conv0 = jnp.zeros((bsz, taps - 1, 3 * w), F32)
        s0 = jnp.zeros((bsz, n_heads, hd, hd), F32)
        xp, kv_p, raw_p, ssm_p = _group(xp, mod_p, False, attn_p, conv0, s0, chunk_p, seq, pw, n_heads)

        xs_pad = jnp.pad(xs, ((0, 0), (0, s_pad - s_new), (0, 0)))

        def attn_s(oattn):
            qkv = oattn.reshape(db, s_pad, 3, n_heads, hd)[:, :s_new]
            qh = qkv[:, :, 0].transpose(0, 2, 1, 3)
            qrows = jax.vmap(lambda t: _split_maps(t, (hd // 2) ** -0.5))(qh.reshape(db, n_heads * s_new, hd))
            knew = qkv[:, :, 1].reshape(db, s_new * n_heads, hd)
            vnew = qkv[:, :, 2].reshape(db, s_new * n_heads, hd)
            o = _sample_attention(qrows, knew, vnew, cache_k, cache_v, l, page_table, slopes, lam_vecs, subg,
                                  lam_init)
            o = o.reshape(db, n_heads, s_new, hd).transpose(0, 2, 1, 3).reshape(db, s_new, w)
            return jnp.pad(o, ((0, 0), (0, s_pad - s_new), (0, 0))).reshape(db * s_pad, w).astype(BF16)

        xs_full, kv_s, raw_s, ssm_s = _group(xs_pad, mod_s, True, attn_s, state_conv[l], state_ssm[l], s_pad,
                                             s_new, pw, n_heads)
        xs = xs_full[:, :s_new]

        raw_p = raw_p.reshape(bsz, seq, 3 * w)
        raw_s = jnp.concatenate([state_conv[l], raw_s.reshape(db, s_pad, 3 * w)[:, :s_new]], axis=1)
        for lst, val in zip(outs, (kv_p[0], kv_p[1], kv_s[0][:, :s_new], kv_s[1][:, :s_new], ssm_p, ssm_s,
                                   raw_p[:, seq - (taps - 1):], raw_s[:, s_new:])):
            lst.append(val)

    return (xp, xs) + tuple(jnp.stack(o) for o in outs)
```

```python
import functools
import math

import jax
import jax.numpy as jnp
from jax import lax
from jax.experimental import pallas as pl
from jax.experimental.pallas import tpu as pltpu

F32 = jnp.float32
BF16 = jnp.bfloat16
EPS = 1e-6
NEG = -1e30
LANES = 128
VMEM_LIMIT = 56 * 1024 * 1024
HI = lax.Precision.HIGHEST


def _cparams(sem):
    return pltpu.CompilerParams(dimension_semantics=sem, vmem_limit_bytes=VMEM_LIMIT)


def _tile(n, target, mult=8):
    if n <= target:
        return n
    for t in range(target, 0, -1):
        if n % t == 0 and t % mult == 0:
            return t
    return n


def _silu(x):
    return x * jax.nn.sigmoid(x)


def _dot(a, b):
    return jnp.dot(a, b, preferred_element_type=F32)


def _split(a):
    hi = a.astype(BF16)
    return hi, (a - hi.astype(F32)).astype(BF16)


def _dot3(a, b):
    ah, al = _split(a)
    bh, bl = _split(b)
    return _dot(ah, bh) + (_dot(ah, bl) + _dot(al, bh))


def _dot_nt(a, b, precision=None):
    return lax.dot_general(a, b, (((1,), (1,)), ((), ())), preferred_element_type=F32,
                           precision=precision)


def _dot_tn(a, b):
    return lax.dot_general(a, b, (((0,), (0,)), ((), ())), preferred_element_type=F32)


def _adaln_kernel(c_ref, w_ref, b_ref, o_ref):
    a = _silu(c_ref[...]).astype(BF16)
    o_ref[...] = _dot(a, w_ref[...].astype(BF16)) + b_ref[...]


def _adaln(c, w, b):
    m, d = c.shape
    n = w.shape[1]
    tn = _tile(n, 1024, LANES)
    return pl.pallas_call(
        _adaln_kernel,
        grid=(n // tn,),
        in_specs=[pl.BlockSpec((m, d), lambda j: (0, 0)),
                  pl.BlockSpec((d, tn), lambda j: (0, j)),
                  pl.BlockSpec((1, tn), lambda j: (0, j))],
        out_specs=pl.BlockSpec((m, tn), lambda j: (0, j)),
        out_shape=jax.ShapeDtypeStruct((m, n), F32),
        compiler_params=_cparams(("arbitrary",)),
        name="adaln",
    )(c, w, b.reshape(1, n))


def _modulated(x, g, shift, scale):
    r = lax.rsqrt(jnp.mean(x * x, axis=-1, keepdims=True) + EPS)
    return ((x * r) * g) * (1.0 + scale) + shift


def _inproj_kernel(x_ref, g_ref, sh_ref, sc_ref, wa_ref, wg_ref, b_ref, ws_ref, bs_ref,
                   oattn_ref, ok_ref, ov_ref, ogdn_ref, ozg_ref, osm_ref, h_ref, *, spw):
    j = pl.program_id(1)
    nj = pl.num_programs(1)
    hps = ok_ref.shape[1] // spw

    @pl.when(j == 0)
    def _():
        hb = _modulated(x_ref[...], g_ref[...], sh_ref[...], sc_ref[...]).astype(BF16)
        h_ref[...] = hb
        osm_ref[...] = _dot_nt(hb, ws_ref[...]) + bs_ref[...]

    def project(w_ref):
        return _dot_nt(h_ref[...], w_ref[...]) + b_ref[...]

    def heads_to_sublanes(y, o_ref, first):
        for hh in range(hps):
            o_ref[:, first + hh, :] = y[:, hh * LANES:(hh + 1) * LANES]

    @pl.when(j < 3 * spw)
    def _():
        y = project(wa_ref)
        oattn_ref[...] = y.astype(BF16)
        for part in range(spw):
            @pl.when(j == spw + part)
            def _():
                heads_to_sublanes(y, ok_ref, part * hps)

            @pl.when(j == 2 * spw + part)
            def _():
                heads_to_sublanes(y, ov_ref, part * hps)

    @pl.when((j >= 3 * spw) & (j < 6 * spw))
    def _():
        ogdn_ref[...] = project(wa_ref)

    @pl.when((j >= 6 * spw) & (j < nj - spw))
    def _():
        ozg_ref[...] = project(wg_ref).astype(BF16)

    @pl.when(j >= nj - spw)
    def _():
        ozg_ref[...] = project(wa_ref).astype(BF16)


def _mod_spec(rows_per_seq, tm, d, chunk, per_row):
    if per_row:
        return pl.BlockSpec((tm, d), lambda i, *_: (i, chunk))
    return pl.BlockSpec((None, 1, d), lambda i, *_: ((i * tm) // rows_per_seq, 0, chunk))


def _inproj(x, mod, rows_per_seq, per_row, g, w_all, w_gates, b_main, w_small, b_small, width):
    r, d = x.shape
    n_heads = width // LANES
    spw = 2 if n_heads % 2 == 0 else 1
    tn = width // spw
    n_gate = w_gates.shape[0] // tn
    nj = 7 * spw + n_gate
    tm = _tile(rows_per_seq if not per_row else r, 1024)
    grid = (r // tm, nj)
    clip = lambda j, lo, hi: jnp.minimum(jnp.maximum(j - lo, 0), hi - lo)
    return pl.pallas_call(
        functools.partial(_inproj_kernel, spw=spw),
        grid=grid,
        in_specs=[pl.BlockSpec((tm, d), lambda i, j: (i, 0), pipeline_mode=pl.Buffered(1)),
                  pl.BlockSpec((1, d), lambda i, j: (0, 0)),
                  _mod_spec(rows_per_seq, tm, d, 0, per_row),
                  _mod_spec(rows_per_seq, tm, d, 1, per_row),
                  pl.BlockSpec((tn, d), lambda i, j: (jnp.where(j < 6 * spw, j, clip(j, nj - spw, nj - 1) + 6 * spw), 0)),
                  pl.BlockSpec((tn, d), lambda i, j: (clip(j, 6 * spw, 6 * spw + n_gate - 1), 0)),
                  pl.BlockSpec((1, tn), lambda i, j: (0, j)),
                  pl.BlockSpec((LANES, d), lambda i, j: (0, 0)),
                  pl.BlockSpec((1, LANES), lambda i, j: (0, 0))],
        out_specs=[pl.BlockSpec((tm, tn), lambda i, j: (i, clip(j, 0, 3 * spw - 1))),
                   pl.BlockSpec((tm, n_heads, LANES), lambda i, j: (i, 0, 0)),
                   pl.BlockSpec((tm, n_heads, LANES), lambda i, j: (i, 0, 0)),
                   pl.BlockSpec((tm, tn), lambda i, j: (i, clip(j, 3 * spw, 6 * spw - 1))),
                   pl.BlockSpec((tm, tn), lambda i, j: (i, clip(j, 6 * spw, nj - 1))),
                   pl.BlockSpec((tm, LANES), lambda i, j: (i, 0))],
        out_shape=[jax.ShapeDtypeStruct((r, 3 * width), BF16),
                   jax.ShapeDtypeStruct((r, n_heads, LANES), F32),
                   jax.ShapeDtypeStruct((r, n_heads, LANES), F32),
                   jax.ShapeDtypeStruct((r, 3 * width), F32),
                   jax.ShapeDtypeStruct((r, (nj - 6 * spw) * tn), BF16),
                   jax.ShapeDtypeStruct((r, LANES), F32)],
        scratch_shapes=[pltpu.VMEM((tm, d), BF16)],
        compiler_params=_cparams(("arbitrary", "arbitrary")),
        name="inproj",
    )(x, g, mod, mod, w_all, w_gates, b_main, w_small, b_small)


def _lambda_value(lq1, lk1, lq2, lk2, lam_init):
    s1 = jnp.sum(lq1 * lk1, axis=-1, keepdims=True)
    s2 = jnp.sum(lq2 * lk2, axis=-1, keepdims=True)
    return jnp.exp(s1) - jnp.exp(s2) + lam_init


def _diff_finish(acc, l, lam, subg, lam_init, half):
    o1 = acc[:half] / l[:half]
    o2 = acc[half:] / l[half:]
    a = o1 - lam * o2
    r = lax.rsqrt(jnp.mean(a * a, axis=-1, keepdims=True) + EPS)
    return ((a * r) * subg) * (1.0 - lam_init)


def _split_maps(q, scale):
    lane = lax.broadcasted_iota(jnp.int32, q.shape, 1)
    qs = q * scale
    zero = jnp.zeros_like(qs)
    half = q.shape[-1] // 2
    return jnp.concatenate([jnp.where(lane < half, qs, zero), jnp.where(lane >= half, qs, zero)], axis=0)


ONES_ROWS = 16


def _pattn_kernel(slopes_ref, lq1_ref, lk1_ref, lq2_ref, lk2_ref, subg_ref, q_ref, k_ref, v_ref,
                  o_ref, vt_ref, bias_ref, m_ref, acc_ref, sa_ref, sb_ref, *, tq, hpb, lam_init, scale):
    hg = pl.program_id(1)
    qi = pl.program_id(2)
    seq = k_ref.shape[0]
    heads = range(hpb)
    lanes = [slice(e * LANES, (e + 1) * LANES) for e in heads]
    slope = [slopes_ref[hg * hpb + e] for e in heads]

    @pl.when(qi == 0)
    def _():
        for e in heads:
            for c in range(seq // tq):
                blk = v_ref[c * tq:(c + 1) * tq, lanes[e]].astype(F32)
                vt_ref[e, 0:LANES, c * tq:(c + 1) * tq] = blk.T.astype(BF16)
            vt_ref[e, LANES:, :] = jnp.ones((ONES_ROWS, seq), BF16)
            bias_ref[e] = slope[e] * lax.broadcasted_iota(jnp.int32, bias_ref.shape[1:], 0).astype(F32)

    qq = [_split_maps(q_ref[:, lanes[e]], scale) for e in heads]
    m_ref[...] = jnp.full(m_ref.shape, NEG, F32)
    acc_ref[...] = jnp.zeros(acc_ref.shape, F32)

    def scores(c, dst_ref):
        start = pl.multiple_of(c * tq, tq)
        for e in heads:
            dst_ref[e] = _dot_nt(k_ref[pl.ds(start, tq), lanes[e]], qq[e])

    def chunk(c, src_ref, masked):
        start = pl.multiple_of(c * tq, tq)
        for e in heads:
            st = src_ref[e] + bias_ref[e]
            if masked:
                kr = lax.broadcasted_iota(jnp.int32, st.shape, 0)
                qc = lax.broadcasted_iota(jnp.int32, st.shape, 1)
                qc = jnp.where(qc >= tq, qc - tq, qc)
                st = jnp.where(qc >= kr, st, NEG)
            m_prev = m_ref[e] - slope[e] * tq
            m_new = jnp.maximum(m_prev, jnp.max(st, axis=0, keepdims=True))
            alpha = jnp.exp(m_prev - m_new)
            p = jnp.exp(st - m_new).astype(BF16)
            acc_ref[e] = alpha * acc_ref[e] + _dot(vt_ref[e, :, pl.ds(start, tq)], p)
            m_ref[e] = m_new

    scores(0, sa_ref)

    def pair(i, carry):
        c = 2 * i
        scores(c + 1, sb_ref)
        chunk(c, sa_ref, False)
        scores(c + 2, sa_ref)
        chunk(c + 1, sb_ref, False)
        return carry

    lax.fori_loop(0, qi // 2, pair, 0)

    @pl.when(qi % 2 == 1)
    def _():
        scores(qi, sb_ref)
        chunk(qi - 1, sa_ref, False)
        chunk(qi, sb_ref, True)

    @pl.when(qi % 2 == 0)
    def _():
        chunk(qi, sa_ref, True)

    lam = _lambda_value(lq1_ref[...], lk1_ref[...], lq2_ref[...], lk2_ref[...], lam_init)
    for e in heads:
        acc = acc_ref[e]
        on = acc[:LANES] / acc[LANES:LANES + 1]
        a = (on[:, :tq] - lam * on[:, tq:]).T
        r = lax.rsqrt(jnp.mean(a * a, axis=-1, keepdims=True) + EPS)
        o_ref[:, lanes[e]] = (((a * r) * subg_ref[...]) * (1.0 - lam_init)).astype(o_ref.dtype)


def _prompt_attention(qkv, slopes, lam_vecs, subg, n_heads, lam_init):
    b, l, _ = qkv.shape
    tq = _tile(l, 256)
    hd = LANES
    hpb = 2 if n_heads % 2 == 0 else 1
    ng = n_heads // hpb
    vec = lambda n: pl.BlockSpec((1, n), lambda b_, h, i: (0, 0))
    kern = functools.partial(_pattn_kernel, tq=tq, hpb=hpb, lam_init=lam_init, scale=(hd // 2) ** -0.5)
    return pl.pallas_call(
        kern,
        grid=(b, ng, l // tq),
        in_specs=[pl.BlockSpec(memory_space=pltpu.SMEM),
                  vec(hd // 2), vec(hd // 2), vec(hd // 2), vec(hd // 2), vec(hd),
                  pl.BlockSpec((None, tq, hpb * hd), lambda b_, h, i: (b_, i, h)),
                  pl.BlockSpec((None, l, hpb * hd), lambda b_, h, i: (b_, 0, ng + h)),
                  pl.BlockSpec((None, l, hpb * hd), lambda b_, h, i: (b_, 0, 2 * ng + h))],
        out_specs=pl.BlockSpec((None, tq, hpb * hd), lambda b_, h, i: (b_, i, h)),
        out_shape=jax.ShapeDtypeStruct((b, l, n_heads * hd), BF16),
        scratch_shapes=[pltpu.VMEM((hpb, hd + ONES_ROWS, l), BF16), pltpu.VMEM((hpb, tq, 2 * tq), F32),
                        pltpu.VMEM((hpb, 1, 2 * tq), F32), pltpu.VMEM((hpb, hd + ONES_ROWS, 2 * tq), F32),
                        pltpu.VMEM((hpb, tq, 2 * tq), F32), pltpu.VMEM((hpb, tq, 2 * tq), F32)],
        compiler_params=_cparams(("arbitrary", "arbitrary", "arbitrary")),
        name="prompt_attn",
    )(slopes, *lam_vecs, subg, qkv, qkv, qkv)


PAGE_RING = 3


def _sattn_kernel(pt_ref, slope_ref, bias_ref, biasn_ref, lq1_ref, lk1_ref, lq2_ref, lk2_ref, subg_ref,
                  q_ref, kn_ref, vn_ref, ck_hbm, cv_hbm, o_ref, kbuf, vbuf, sem, m_ref, l_ref, acc_ref,
                  *, layer, n_pages_step, n_seq, ng, page, lam_init):
    b = pl.program_id(0)
    g = pl.program_id(1)
    n = b * ng + g
    total = n_seq * ng

    def group_copies(seq, grp, slot):
        copies = []
        for i in range(n_pages_step):
            pid = pt_ref[seq, grp * n_pages_step + i]
            copies.append(pltpu.make_async_copy(ck_hbm.at[layer, pid], kbuf.at[slot, i], sem.at[0, slot]))
            copies.append(pltpu.make_async_copy(cv_hbm.at[layer, pid], vbuf.at[slot, i], sem.at[1, slot]))
        return copies

    @pl.when(n == 0)
    def _():
        for d in range(min(PAGE_RING - 1, total)):
            for cp in group_copies(d // ng, d % ng, d):
                cp.start()

    nxt = n + (PAGE_RING - 1)

    @pl.when(nxt < total)
    def _():
        for cp in group_copies(lax.div(nxt, ng), lax.rem(nxt, ng), lax.rem(nxt, PAGE_RING)):
            cp.start()

    slot = lax.rem(n, PAGE_RING)
    for cp in group_copies(b, g, slot):
        cp.wait()
    k_refs = [kbuf.at[slot, i] for i in range(n_pages_step)]
    v_refs = [vbuf.at[slot, i] for i in range(n_pages_step)]

    @pl.when(g == 0)
    def _():
        m_ref[...] = jnp.full(m_ref.shape, NEG, F32)
        l_ref[...] = jnp.zeros(l_ref.shape, F32)
        acc_ref[...] = jnp.zeros(acc_ref.shape, F32)

    qq = q_ref[...]
    step = slope_ref[...] * (page * n_pages_step)

    def update(s, vs):
        m_prev = m_ref[...] - step
        m_new = jnp.maximum(m_prev, jnp.max(s, axis=-1, keepdims=True))
        alpha = jnp.exp(m_prev - m_new)
        p = jnp.exp(s - m_new)
        l_ref[...] = alpha * l_ref[...] + jnp.sum(p, axis=-1, keepdims=True)
        pb = p.astype(BF16)
        pv = None
        off = 0
        for vf in vs:
            part = _dot(pb[:, off:off + vf.shape[0]], vf)
            pv = part if pv is None else pv + part
            off += vf.shape[0]
        acc_ref[...] = alpha * acc_ref[...] + pv
        m_ref[...] = m_new

    s = jnp.concatenate([_dot_nt(qq, k_refs[i][...].reshape(-1, LANES).astype(BF16))
                         for i in range(n_pages_step)], axis=1) + bias_ref[...]
    update(s, [v_refs[i][...].reshape(-1, LANES).astype(BF16) for i in range(n_pages_step)])

    @pl.when(g == ng - 1)
    def _():
        update(_dot_nt(qq, kn_ref[...]) + biasn_ref[...], [vn_ref[...]])
        lam = _lambda_value(lq1_ref[...], lk1_ref[...], lq2_ref[...], lk2_ref[...], lam_init)
        half = acc_ref.shape[0] // 2
        o_ref[...] = _diff_finish(acc_ref[...], l_ref[...], lam, subg_ref[...], lam_init, half)


def _sample_attention(qrows, knew, vnew, cache_k, cache_v, layer, page_table, slopes, lam_vecs, subg, lam_init):
    db, nrow, hd = qrows.shape
    _, n_phys, page, n_heads, _ = cache_k.shape
    s_new = knew.shape[1] // n_heads
    n_pages = page_table.shape[1]
    pps = 8 if n_pages % 8 == 0 else 1
    r = jnp.arange(nrow)
    r_head = (r // s_new) % n_heads
    r_tok = r % s_new
    slope_rows = slopes[r_head][:, None]
    c = jnp.arange(pps * page * n_heads)
    bias = jnp.where(r_head[:, None] == (c % n_heads)[None, :], slope_rows * (c // n_heads)[None, :].astype(F32), NEG)
    cn = jnp.arange(s_new * n_heads)
    ok = (r_head[:, None] == (cn % n_heads)[None, :]) & ((cn // n_heads)[None, :] <= r_tok[:, None])
    biasn = jnp.where(ok, slope_rows * (cn // n_heads)[None, :].astype(F32), NEG)

    const = lambda shape: pl.BlockSpec(shape, lambda b, g, pt: (0,) * len(shape))
    kern = functools.partial(_sattn_kernel, layer=layer, n_pages_step=pps, n_seq=db, ng=n_pages // pps,
                             page=float(page), lam_init=lam_init)
    ring = (PAGE_RING, pps, page, n_heads, hd)
    grid_spec = pltpu.PrefetchScalarGridSpec(
        num_scalar_prefetch=1,
        grid=(db, n_pages // pps),
        in_specs=[const((nrow, 1)), const((nrow, pps * page * n_heads)), const((nrow, s_new * n_heads)),
                  const((1, hd // 2)), const((1, hd // 2)), const((1, hd // 2)), const((1, hd // 2)), const((1, hd)),
                  pl.BlockSpec((None, nrow, hd), lambda b, g, pt: (b, 0, 0)),
                  pl.BlockSpec((None, s_new * n_heads, hd), lambda b, g, pt: (b, 0, 0)),
                  pl.BlockSpec((None, s_new * n_heads, hd), lambda b, g, pt: (b, 0, 0)),
                  pl.BlockSpec(memory_space=pl.ANY), pl.BlockSpec(memory_space=pl.ANY)],
        out_specs=pl.BlockSpec((None, nrow // 2, hd), lambda b, g, pt: (b, 0, 0)),
        scratch_shapes=[pltpu.VMEM(ring, F32), pltpu.VMEM(ring, F32), pltpu.SemaphoreType.DMA((2, PAGE_RING)),
                        pltpu.VMEM((nrow, 1), F32), pltpu.VMEM((nrow, 1), F32), pltpu.VMEM((nrow, hd), F32)],
    )
    return pl.pallas_call(
        kern,
        grid_spec=grid_spec,
        out_shape=jax.ShapeDtypeStruct((db, nrow // 2, hd), F32),
        compiler_params=_cparams(("arbitrary", "arbitrary")),
        name="sample_attn",
    )(page_table, slope_rows, bias, biasn, *lam_vecs, subg, qrows, knew, vnew, cache_k, cache_v)


def _gdn_prep_kernel(x_ref, sm_ref, st_ref, cw_ref, alog_ref, dtb_ref, q_ref, k_ref, v_ref, gate_ref, pad_ref,
                     *, tt, n_heads, valid_len, taps):
    t = pl.program_id(1)
    halo = 8
    w3 = n_heads * LANES

    @pl.when(t == 0)
    def _():
        pad_ref[0:halo, :] = jnp.zeros((halo, pad_ref.shape[1]), F32)
        pad_ref[halo - (taps - 1):halo, :] = st_ref[...]

    pad_ref[halo:halo + tt, :] = x_ref[...]
    y = pad_ref[halo:halo + tt, :] * cw_ref[taps - 1:taps, :]
    for j in range(taps - 1):
        off = halo - (taps - 1) + j
        y = y + pad_ref[off:off + tt, :] * cw_ref[j:j + 1, :]
    y = _silu(y)
    pad_ref[0:halo, :] = pad_ref[tt:tt + halo, :]

    def l2n(z):
        return z * lax.rsqrt(jnp.sum(z * z, axis=-1, keepdims=True) + EPS)

    for hh in range(n_heads):
        sl = slice(hh * LANES, (hh + 1) * LANES)
        q_ref[:, sl] = l2n(y[:, sl]) * (LANES ** -0.5)
        k_ref[:, sl] = l2n(y[:, w3 + hh * LANES:w3 + (hh + 1) * LANES])
    v_ref[...] = y[:, 2 * w3:]

    x = sm_ref[...]
    lane = lax.broadcasted_iota(jnp.int32, x.shape, 1)
    row = lax.broadcasted_iota(jnp.int32, x.shape, 0) + t * tt
    beta = jax.nn.sigmoid(x)
    z = x + dtb_ref[...]
    softplus = jnp.maximum(z, 0.0) + jnp.log(1.0 + jnp.exp(-jnp.abs(z)))
    gdec = -jnp.exp(alog_ref[...]) * softplus
    out = jnp.where(lane < n_heads, beta, gdec)
    gate_ref[...] = jnp.where((row < valid_len) & (lane < 2 * n_heads), out, 0.0)


def _gdn_prep(x, small, state, conv_w, a_log, dt_bias, n_heads, valid_len):
    b, l, ch = x.shape
    taps = conv_w.shape[0]
    tt = _tile(l, 256)
    w = n_heads * LANES
    lanes = jnp.arange(LANES)
    in_g = (lanes >= n_heads) & (lanes < 2 * n_heads)
    alog_v = jnp.where(in_g, a_log[jnp.clip(lanes - n_heads, 0, n_heads - 1)], 0.0).reshape(1, LANES)
    dtb_v = jnp.where(in_g, dt_bias[jnp.clip(lanes - n_heads, 0, n_heads - 1)], 0.0).reshape(1, LANES)
    kern = functools.partial(_gdn_prep_kernel, tt=tt, n_heads=n_heads, valid_len=valid_len, taps=taps)
    row_spec = lambda n: pl.BlockSpec((None, tt, n), lambda b_, t: (b_, t, 0))
    return pl.pallas_call(
        kern,
        grid=(b, l // tt),
        in_specs=[row_spec(ch), row_spec(LANES),
                  pl.BlockSpec((None, taps - 1, ch), lambda b_, t: (b_, 0, 0)),
                  pl.BlockSpec((taps, ch), lambda b_, t: (0, 0)),
                  pl.BlockSpec((1, LANES), lambda b_, t: (0, 0)),
                  pl.BlockSpec((1, LANES), lambda b_, t: (0, 0))],
        out_specs=[row_spec(w), row_spec(w), row_spec(w), row_spec(LANES)],
        out_shape=[jax.ShapeDtypeStruct((b, l, w), F32)] * 3 + [jax.ShapeDtypeStruct((b, l, LANES), F32)],
        scratch_shapes=[pltpu.VMEM((tt + 8, ch), F32)],
        compiler_params=_cparams(("arbitrary", "arbitrary")),
        name="gdn_prep",
    )(x, small, state, conv_w, alog_v, dtb_v)


def _gdn_kernel(q_ref, k_ref, v_ref, gate_ref, z_ref, gn_ref, s0_ref, o_ref, s_ref, *, chunk, n_chunks, n_heads):
    t = pl.program_id(1)

    @pl.when(t == 0)
    def _():
        s_ref[...] = s0_ref[...]

    c_ = chunk
    ri = lax.broadcasted_iota(jnp.int32, (c_, c_), 0)
    ci = lax.broadcasted_iota(jnp.int32, (c_, c_), 1)
    causal = ri >= ci
    strict = ri > ci
    tri = jnp.where(causal, 1.0, 0.0).astype(F32)
    eye = jnp.where(ri == ci, 1.0, 0.0).astype(F32)
    n_sq = max(int(math.log2(c_)) - 1, 0)
    gn = gn_ref[...]
    sel_r = lax.broadcasted_iota(jnp.int32, (8, LANES), 0)
    sel_c = lax.broadcasted_iota(jnp.int32, (8, LANES), 1)
    sel = jnp.where(sel_c == sel_r + n_heads, 1.0, 0.0).astype(F32)
    zpad = jnp.zeros((LANES - c_, LANES), F32)

    def one_chunk(c, carry):
        rows = pl.ds(c * c_ if n_chunks == 1 else pl.multiple_of(c * c_, c_), c_)
        gates = gate_ref[rows, :]
        gcum = jnp.dot(tri, gates, preferred_element_type=F32, precision=HI)
        gcum_t = _dot_nt(sel, gcum, precision=HI)
        hs = range(n_heads)
        sl = [slice(hh * LANES, (hh + 1) * LANES) for hh in hs]
        q = [q_ref[rows, sl[hh]] for hh in hs]
        k = [k_ref[rows, sl[hh]] for hh in hs]
        beta = [gates[:, hh:hh + 1] for hh in hs]
        gc = [gcum[:, n_heads + hh:n_heads + hh + 1] for hh in hs]
        glast = [gcum[c_ - 1:c_, n_heads + hh:n_heads + hh + 1] for hh in hs]
        decay = [jnp.exp(jnp.where(causal, gc[hh] - gcum_t[hh:hh + 1, :], NEG)) for hh in hs]
        kb = [k[hh].astype(BF16) for hh in hs]
        qk_kk = [_dot_nt(jnp.concatenate([kb[hh], q[hh].astype(BF16)], axis=0), kb[hh]) for hh in hs]
        a = [jnp.where(strict, beta[hh] * qk_kk[hh][:c_] * decay[hh], 0.0) for hh in hs]
        tinv = [eye - a[hh] for hh in hs]
        ap = a
        for _ in range(n_sq):
            ap = [_dot3(ap[hh], ap[hh]) for hh in hs]
            tinv = [tinv[hh] + _dot3(tinv[hh], ap[hh]) for hh in hs]
        eg = [jnp.exp(gc[hh]) for hh in hs]
        rhs = [jnp.concatenate([v_ref[rows, sl[hh]] * beta[hh], k[hh] * (beta[hh] * eg[hh])], axis=1).astype(BF16)
               for hh in hs]
        uw = [_dot(tinv[hh].astype(BF16), rhs[hh]) for hh in hs]
        s = [s_ref[hh] for hh in hs]
        wq = [_dot(jnp.concatenate([uw[hh][:, LANES:], q[hh] * eg[hh]], axis=0).astype(BF16), s[hh].astype(BF16))
              for hh in hs]
        u = [uw[hh][:, :LANES] - wq[hh][:c_] for hh in hs]
        ub = [u[hh].astype(BF16) for hh in hs]
        o = [wq[hh][c_:] + _dot((qk_kk[hh][c_:] * decay[hh]).astype(BF16), ub[hh]) for hh in hs]
        for hh in hs:
            kd = k[hh] * jnp.exp(glast[hh] - gc[hh])
            kd_t = jnp.concatenate([kd, zpad], axis=0).T.astype(BF16)
            u_pad = jnp.concatenate([u[hh], zpad], axis=0).astype(BF16)
            s_ref[hh] = s[hh] * jnp.exp(glast[hh]) + _dot(kd_t, u_pad)
        for hh in hs:
            r = lax.rsqrt(jnp.mean(o[hh] * o[hh], axis=-1, keepdims=True) + EPS)
            o_ref[rows, sl[hh]] = (((o[hh] * r) * gn) * _silu(z_ref[rows, sl[hh]].astype(F32))).astype(o_ref.dtype)
        return carry

    if n_chunks == 1:
        one_chunk(0, 0)
    else:
        lax.fori_loop(0, n_chunks, one_chunk, 0)


def _gdn(q, k, v, gates, z_arr, z_block, gn, s0, chunk, n_heads):
    b, l, w = q.shape
    tt = _tile(l, max(chunk, 256), chunk)
    kern = functools.partial(_gdn_kernel, chunk=chunk, n_chunks=tt // chunk, n_heads=n_heads)
    row_spec = lambda n: pl.BlockSpec((None, tt, n), lambda b_, t: (b_, t, 0))
    st_spec = pl.BlockSpec((None, n_heads, LANES, LANES), lambda b_, t: (b_, 0, 0, 0))
    return pl.pallas_call(
        kern,
        grid=(b, l // tt),
        in_specs=[row_spec(w), row_spec(w), row_spec(w), row_spec(LANES),
                  pl.BlockSpec((None, tt, w), lambda b_, t: (b_, t, z_block)),
                  pl.BlockSpec((1, LANES), lambda b_, t: (0, 0)),
                  st_spec],
        out_specs=[row_spec(w), st_spec],
        out_shape=[jax.ShapeDtypeStruct((b, l, w), BF16),
                   jax.ShapeDtypeStruct(s0.shape, F32)],
        compiler_params=_cparams(("arbitrary", "arbitrary")),
        name="gdn",
    )(q, k, v, gates, z_arr, gn, s0)


def _merge_kernel(a_ref, g_ref, ga_ref, gg_ref, wpa_ref, wpg_ref, wo_ref, x_ref, gate_ref, gn_ref, o_ref):
    pa = _dot(a_ref[...], wpa_ref[...])
    pg = _dot(g_ref[...], wpg_ref[...])
    m = jax.nn.sigmoid(ga_ref[...].astype(F32)) * pa + jax.nn.sigmoid(gg_ref[...].astype(F32)) * pg
    y = _dot(m.astype(BF16), wo_ref[...])
    r = lax.rsqrt(jnp.mean(y * y, axis=-1, keepdims=True) + EPS)
    o_ref[...] = x_ref[...] + gate_ref[...] * ((y * r) * gn_ref[...])


def _merge(a_out, g_out, zg, x, mod, rows_per_seq, per_row, w_pa, w_pg, w_o, gn):
    r, d = x.shape
    w = a_out.shape[1]
    tm = _tile(rows_per_seq if not per_row else r, 256)
    resident = lambda shape: pl.BlockSpec(shape, lambda i: (0, 0), pipeline_mode=pl.Buffered(1))
    return pl.pallas_call(
        _merge_kernel,
        grid=(r // tm,),
        in_specs=[pl.BlockSpec((tm, w), lambda i: (i, 0)),
                  pl.BlockSpec((tm, w), lambda i: (i, 0)),
                  pl.BlockSpec((tm, d), lambda i: (i, 0)),
                  pl.BlockSpec((tm, d), lambda i: (i, 1)),
                  resident((w, d)), resident((w, d)), resident((d, d)),
                  pl.BlockSpec((tm, d), lambda i: (i, 0)),
                  _mod_spec(rows_per_seq, tm, d, 2, per_row),
                  pl.BlockSpec((1, d), lambda i: (0, 0))],
        out_specs=pl.BlockSpec((tm, d), lambda i: (i, 0)),
        out_shape=jax.ShapeDtypeStruct((r, d), F32),
        compiler_params=_cparams(("arbitrary",)),
        name="merge",
    )(a_out, g_out, zg, zg, w_pa, w_pg, w_o, x, mod, gn)


def _ffn_up_kernel(x_ref, g_ref, sh_ref, sc_ref, wg_ref, wu_ref, o_ref, h_ref):
    @pl.when(pl.program_id(1) == 0)
    def _():
        h_ref[...] = _modulated(x_ref[...], g_ref[...], sh_ref[...], sc_ref[...]).astype(BF16)

    h = h_ref[...]
    gt = _dot(h, wg_ref[...])
    up = _dot(h, wu_ref[...])
    o_ref[...] = (_silu(gt) * up).astype(o_ref.dtype)


def _ffn_up(x, mod, rows_per_seq, per_row, g, w_up):
    r, d = x.shape
    f = w_up.shape[1] // 2
    tn = _tile(f, 512, LANES)
    nj = f // tn
    tm = _tile(rows_per_seq if not per_row else r, 1024)
    return pl.pallas_call(
        _ffn_up_kernel,
        grid=(r // tm, nj),
        in_specs=[pl.BlockSpec((tm, d), lambda i, j: (i, 0)),
                  pl.BlockSpec((1, d), lambda i, j: (0, 0)),
                  _mod_spec(rows_per_seq, tm, d, 3, per_row),
                  _mod_spec(rows_per_seq, tm, d, 4, per_row),
                  pl.BlockSpec((d, tn), lambda i, j: (0, j)),
                  pl.BlockSpec((d, tn), lambda i, j: (0, nj + j))],
        out_specs=pl.BlockSpec((tm, tn), lambda i, j: (i, j)),
        out_shape=jax.ShapeDtypeStruct((r, f), BF16),
        scratch_shapes=[pltpu.VMEM((tm, d), BF16)],
        compiler_params=_cparams(("arbitrary", "arbitrary")),
        name="ffn_up",
    )(x, g, mod, mod, w_up, w_up)


def _ffn_down_kernel(a_ref, w_ref, x_ref, gate_ref, gn_ref, o_ref):
    y = _dot(a_ref[...], w_ref[...])
    r = lax.rsqrt(jnp.mean(y * y, axis=-1, keepdims=True) + EPS)
    o_ref[...] = x_ref[...] + gate_ref[...] * ((y * r) * gn_ref[...])


def _ffn_down(act, x, mod, rows_per_seq, per_row, w_down, gn):
    r, d = x.shape
    f = act.shape[1]
    tm = _tile(rows_per_seq if not per_row else r, 256)
    return pl.pallas_call(
        _ffn_down_kernel,
        grid=(r // tm,),
        in_specs=[pl.BlockSpec((tm, f), lambda i: (i, 0)),
                  pl.BlockSpec((f, d), lambda i: (0, 0), pipeline_mode=pl.Buffered(1)),
                  pl.BlockSpec((tm, d), lambda i: (i, 0)),
                  _mod_spec(rows_per_seq, tm, d, 5, per_row),
                  pl.BlockSpec((1, d), lambda i: (0, 0))],
        out_specs=pl.BlockSpec((tm, d), lambda i: (i, 0)),
        out_shape=jax.ShapeDtypeStruct((r, d), F32),
        compiler_params=_cparams(("arbitrary",)),
        name="ffn_down",
    )(act, w_down, x, mod, gn)


def _group(x, mod, per_row, attn_fn, conv_state, s0, chunk, valid_len, pw, n_heads):
    b, l, d = x.shape
    r = b * l
    w = n_heads * LANES
    xf = x.reshape(r, d)
    oattn, ok, ov, ogdn, ozg, osm = _inproj(xf, mod, l, per_row, pw["g_mix_pre"], pw["w_all"], pw["w_gates"],
                                            pw["b_main"], pw["w_small"], pw["b_small"], w)
    a_out = attn_fn(oattn)
    q, k, v, gates = _gdn_prep(ogdn.reshape(b, l, 3 * w), osm.reshape(b, l, LANES), conv_state,
                               pw["conv_w"], pw["a_log"], pw["dt_bias"], n_heads, valid_len)
    g_out, s_new = _gdn(q, k, v, gates, ozg.reshape(b, l, -1), (2 * d) // w, pw["gdn_norm"], s0, chunk, n_heads)
    x1 = _merge(a_out, g_out.reshape(r, w), ozg, xf, mod, l, per_row, pw["w_pa"], pw["w_pg"], pw["w_o"],
                pw["g_mix_post"])
    act = _ffn_up(x1, mod, l, per_row, pw["g_ffn_pre"], pw["w_up"])
    x2 = _ffn_down(act, x1, mod, l, per_row, pw["w_down"], pw["g_ffn_post"])
    kv = (ok.reshape(b, l, n_heads, LANES), ov.reshape(b, l, n_heads, LANES))
    return x2.reshape(b, l, d), kv, ogdn, s_new


def kernel(x_prompt, x_sample, c_prompt, c_sample, cache_k, cache_v, page_table, state_ssm, state_conv, w_ada, b_ada, norm_mix_pre, norm_mix_post, norm_ffn_pre, norm_ffn_post, w_in, b_in, conv_w, lambda_q1, lambda_k1, lambda_q2, lambda_k2, attn_subln, a_log, dt_bias, gdn_norm, w_proj_attn, w_proj_gdn, w_out, w_ffn_up, w_ffn_down):
    bsz, seq, d = x_prompt.shape
    db, s_new, _ = x_sample.shape
    depth = w_in.shape[0]
    n_heads = cache_k.shape[3]
    hd = cache_k.shape[4]
    assert hd == LANES and state_ssm.shape[-1] == LANES and state_ssm.shape[-2] == LANES
    assert state_ssm.shape[2] == n_heads
    w = n_heads * hd
    taps = conv_w.shape[1]
    chunk_p = 64 if seq % 64 == 0 else seq
    slopes = 2.0 ** (-8.0 * jnp.arange(1, n_heads + 1, dtype=F32) / n_heads)
    assert (2 * d) % w == 0
    s_pad = -(-s_new // 8) * 8

    xp, xs = x_prompt, x_sample
    outs = [[] for _ in range(8)]
    for l in range(depth):
        lam_init = 0.8 - 0.6 * math.exp(-0.3 * l)
        lam_vecs = [t[l].reshape(1, -1) for t in (lambda_q1, lambda_k1, lambda_q2, lambda_k2)]
        subg = attn_subln[l].reshape(1, hd)

        c_all = jnp.concatenate([c_prompt, c_sample], axis=0)
        c_all = jnp.pad(c_all, ((0, (-c_all.shape[0]) % 8), (0, 0)))
        mod = _adaln(c_all, w_ada[l], b_ada[l])
        mod_p = mod[:bsz].reshape(bsz, 1, 6 * d)
        mod_s = jnp.repeat(mod[bsz:bsz + db], s_pad, axis=0)

        wt, bi = w_in[l].T, b_in[l]
        n6 = 6 * w
        n7 = 7 * w
        w_all = wt.astype(BF16)
        pw = dict(
            w_all=w_all, w_gates=w_all[n7 + 2 * n_heads:],
            b_main=jnp.concatenate([bi[:n6], bi[n7 + 2 * n_heads:], bi[n6:n7]]).reshape(1, -1),
            w_small=jnp.pad(w_all[n7:n7 + 2 * n_heads], ((0, LANES - 2 * n_heads), (0, 0))),
            b_small=jnp.pad(bi[n7:n7 + 2 * n_heads], (0, LANES - 2 * n_heads)).reshape(1, LANES),
            g_mix_pre=norm_mix_pre[l].reshape(1, d), g_mix_post=norm_mix_post[l].reshape(1, d),
            g_ffn_pre=norm_ffn_pre[l].reshape(1, d), g_ffn_post=norm_ffn_post[l].reshape(1, d),
            conv_w=conv_w[l], a_log=a_log[l], dt_bias=dt_bias[l], gdn_norm=gdn_norm[l].reshape(1, hd),
            w_pa=w_proj_attn[l].astype(BF16), w_pg=w_proj_gdn[l].astype(BF16), w_o=w_out[l].astype(BF16),
            w_up=w_ffn_up[l].astype(BF16), w_down=w_ffn_down[l].astype(BF16),
        )

        def attn_p(oattn):
            o = _prompt_attention(oattn.reshape(bsz, seq, 3 * w), slopes, lam_vecs, subg, n_heads, lam_init)
            return o.reshape(bsz * seq, w)

        conv0 = jnp.zeros((bsz, taps - 1, 3 * w), F32)
        s0 = jnp.zeros((bsz, n_heads, hd, hd), F32)
        xp, kv_p, raw_p, ssm_p = _group(xp, mod_p, False, attn_p, conv0, s0, chunk_p, seq, pw, n_heads)

        xs_pad = jnp.pad(xs, ((0, 0), (0, s_pad - s_new), (0, 0)))

        def attn_s(oattn):
            qkv = oattn.reshape(db, s_pad, 3, n_heads, hd)[:, :s_new]
            qh = qkv[:, :, 0].transpose(0, 2, 1, 3)
            qrows = jax.vmap(lambda t: _split_maps(t, (hd // 2) ** -0.5))(qh.reshape(db, n_heads * s_new, hd))
            knew = qkv[:, :, 1].reshape(db, s_new * n_heads, hd)
            vnew = qkv[:, :, 2].reshape(db, s_new * n_heads, hd)
            o = _sample_attention(qrows, knew, vnew, cache_k, cache_v, l, page_table, slopes, lam_vecs, subg,
                                  lam_init)
            o = o.reshape(db, n_heads, s_new, hd).transpose(0, 2, 1, 3).reshape(db, s_new, w)
            return jnp.pad(o, ((0, 0), (0, s_pad - s_new), (0, 0))).reshape(db * s_pad, w).astype(BF16)

        xs_full, kv_s, raw_s, ssm_s = _group(xs_pad, mod_s, True, attn_s, state_conv[l], state_ssm[l], s_pad,
                                             s_new, pw, n_heads)
        xs = xs_full[:, :s_new]

        raw_p = raw_p.reshape(bsz, seq, 3 * w)
        raw_s = jnp.concatenate([state_conv[l], raw_s.reshape(db, s_pad, 3 * w)[:, :s_new]], axis=1)
        for lst, val in zip(outs, (kv_p[0], kv_p[1], kv_s[0][:, :s_new], kv_s[1][:, :s_new], ssm_p, ssm_s,
                                   raw_p[:, seq - (taps - 1):], raw_s[:, s_new:])):
            lst.append(val)

    return (xp, xs) + tuple(jnp.stack(o) for o in outs)
```

```python
import functools
import math

import jax
import jax.numpy as jnp
from jax import lax
from jax.experimental import pallas as pl
from jax.experimental.pallas import tpu as pltpu

F32 = jnp.float32
BF16 = jnp.bfloat16
EPS = 1e-6
NEG = -1e30
LANES = 128
VMEM_LIMIT = 56 * 1024 * 1024
HI = lax.Precision.HIGHEST


def _cparams(sem):
    return pltpu.CompilerParams(dimension_semantics=sem, vmem_limit_bytes=VMEM_LIMIT)


def _tile(n, target, mult=8):
    if n <= target:
        return n
    for t in range(target, 0, -1):
        if n % t == 0 and t % mult == 0:
            return t
    return n


def _silu(x):
    return x * jax.nn.sigmoid(x)


def _dot(a, b):
    return jnp.dot(a, b, preferred_element_type=F32)


def _split(a):
    hi = a.astype(BF16)
    return hi, (a - hi.astype(F32)).astype(BF16)


def _dot3(a, b):
    ah, al = _split(a)
    bh, bl = _split(b)
    return _dot(ah, bh) + (_dot(ah, bl) + _dot(al, bh))


def _dot_nt(a, b, precision=None):
    return lax.dot_general(a, b, (((1,), (1,)), ((), ())), preferred_element_type=F32,
                           precision=precision)


def _dot_tn(a, b):
    return lax.dot_general(a, b, (((0,), (0,)), ((), ())), preferred_element_type=F32)


def _adaln_kernel(c_ref, w_ref, b_ref, o_ref):
    a = _silu(c_ref[...]).astype(BF16)
    o_ref[...] = _dot(a, w_ref[...].astype(BF16)) + b_ref[...]


def _adaln(c, w, b):
    m, d = c.shape
    n = w.shape[1]
    tn = _tile(n, 1024, LANES)
    return pl.pallas_call(
        _adaln_kernel,
        grid=(n // tn,),
        in_specs=[pl.BlockSpec((m, d), lambda j: (0, 0)),
                  pl.BlockSpec((d, tn), lambda j: (0, j)),
                  pl.BlockSpec((1, tn), lambda j: (0, j))],
        out_specs=pl.BlockSpec((m, tn), lambda j: (0, j)),
        out_shape=jax.ShapeDtypeStruct((m, n), F32),
        compiler_params=_cparams(("arbitrary",)),
        name="adaln",
    )(c, w, b.reshape(1, n))


def _modulated(x, g, shift, scale):
    r = lax.rsqrt(jnp.mean(x * x, axis=-1, keepdims=True) + EPS)
    return ((x * r) * g) * (1.0 + scale) + shift


def _inproj_kernel(x_ref, g_ref, sh_ref, sc_ref, wa_ref, wg_ref, b_ref, ws_ref, bs_ref,
                   oattn_ref, ok_ref, ov_ref, ogdn_ref, ozg_ref, osm_ref, h_ref, *, spw):
    j = pl.program_id(1)
    nj = pl.num_programs(1)
    hps = ok_ref.shape[1] // spw

    @pl.when(j == 0)
    def _():
        hb = _modulated(x_ref[...], g_ref[...], sh_ref[...], sc_ref[...]).astype(BF16)
        h_ref[...] = hb
        osm_ref[...] = _dot_nt(hb, ws_ref[...]) + bs_ref[...]

    def project(w_ref):
        return _dot_nt(h_ref[...], w_ref[...]) + b_ref[...]

    def heads_to_sublanes(y, o_ref, first):
        for hh in range(hps):
            o_ref[:, first + hh, :] = y[:, hh * LANES:(hh + 1) * LANES]

    @pl.when(j < 3 * spw)
    def _():
        y = project(wa_ref)
        oattn_ref[...] = y.astype(BF16)
        for part in range(spw):
            @pl.when(j == spw + part)
            def _():
                heads_to_sublanes(y, ok_ref, part * hps)

            @pl.when(j == 2 * spw + part)
            def _():
                heads_to_sublanes(y, ov_ref, part * hps)

    @pl.when((j >= 3 * spw) & (j < 6 * spw))
    def _():
        ogdn_ref[...] = project(wa_ref)

    @pl.when((j >= 6 * spw) & (j < nj - spw))
    def _():
        ozg_ref[...] = project(wg_ref).astype(BF16)

    @pl.when(j >= nj - spw)
    def _():
        ozg_ref[...] = project(wa_ref).astype(BF16)


def _mod_spec(rows_per_seq, tm, d, chunk, per_row):
    if per_row:
        return pl.BlockSpec((tm, d), lambda i, *_: (i, chunk))
    return pl.BlockSpec((None, 1, d), lambda i, *_: ((i * tm) // rows_per_seq, 0, chunk))


def _inproj(x, mod, rows_per_seq, per_row, g, w_all, w_gates, b_main, w_small, b_small, width):
    r, d = x.shape
    n_heads = width // LANES
    tm = _tile(rows_per_seq if not per_row else r, 1024)
    spw = 2 if (n_heads % 2 == 0 and tm > 512) else 1
    tn = width // spw
    n_gate = w_gates.shape[0] // tn
    nj = 7 * spw + n_gate
    grid = (r // tm, nj)
    clip = lambda j, lo, hi: jnp.minimum(jnp.maximum(j - lo, 0), hi - lo)
    return pl.pallas_call(
        functools.partial(_inproj_kernel, spw=spw),
        grid=grid,
        in_specs=[pl.BlockSpec((tm, d), lambda i, j: (i, 0), pipeline_mode=pl.Buffered(1)),
                  pl.BlockSpec((1, d), lambda i, j: (0, 0)),
                  _mod_spec(rows_per_seq, tm, d, 0, per_row),
                  _mod_spec(rows_per_seq, tm, d, 1, per_row),
                  pl.BlockSpec((tn, d), lambda i, j: (jnp.where(j < 6 * spw, j, clip(j, nj - spw, nj - 1) + 6 * spw), 0)),
                  pl.BlockSpec((tn, d), lambda i, j: (clip(j, 6 * spw, 6 * spw + n_gate - 1), 0)),
                  pl.BlockSpec((1, tn), lambda i, j: (0, j)),
                  pl.BlockSpec((LANES, d), lambda i, j: (0, 0)),
                  pl.BlockSpec((1, LANES), lambda i, j: (0, 0))],
        out_specs=[pl.BlockSpec((tm, tn), lambda i, j: (i, clip(j, 0, 3 * spw - 1))),
                   pl.BlockSpec((tm, n_heads, LANES), lambda i, j: (i, 0, 0)),
                   pl.BlockSpec((tm, n_heads, LANES), lambda i, j: (i, 0, 0)),
                   pl.BlockSpec((tm, tn), lambda i, j: (i, clip(j, 3 * spw, 6 * spw - 1))),
                   pl.BlockSpec((tm, tn), lambda i, j: (i, clip(j, 6 * spw, nj - 1))),
                   pl.BlockSpec((tm, LANES), lambda i, j: (i, 0))],
        out_shape=[jax.ShapeDtypeStruct((r, 3 * width), BF16),
                   jax.ShapeDtypeStruct((r, n_heads, LANES), F32),
                   jax.ShapeDtypeStruct((r, n_heads, LANES), F32),
                   jax.ShapeDtypeStruct((r, 3 * width), F32),
                   jax.ShapeDtypeStruct((r, (nj - 6 * spw) * tn), BF16),
                   jax.ShapeDtypeStruct((r, LANES), F32)],
        scratch_shapes=[pltpu.VMEM((tm, d), BF16)],
        compiler_params=_cparams(("arbitrary", "arbitrary")),
        name="inproj",
    )(x, g, mod, mod, w_all, w_gates, b_main, w_small, b_small)


def _lambda_value(lq1, lk1, lq2, lk2, lam_init):
    s1 = jnp.sum(lq1 * lk1, axis=-1, keepdims=True)
    s2 = jnp.sum(lq2 * lk2, axis=-1, keepdims=True)
    return jnp.exp(s1) - jnp.exp(s2) + lam_init


def _diff_finish(acc, l, lam, subg, lam_init, half):
    o1 = acc[:half] / l[:half]
    o2 = acc[half:] / l[half:]
    a = o1 - lam * o2
    r = lax.rsqrt(jnp.mean(a * a, axis=-1, keepdims=True) + EPS)
    return ((a * r) * subg) * (1.0 - lam_init)


def _split_maps(q, scale):
    lane = lax.broadcasted_iota(jnp.int32, q.shape, 1)
    qs = q * scale
    zero = jnp.zeros_like(qs)
    half = q.shape[-1] // 2
    return jnp.concatenate([jnp.where(lane < half, qs, zero), jnp.where(lane >= half, qs, zero)], axis=0)


ONES_ROWS = 16


def _pattn_kernel(slopes_ref, lq1_ref, lk1_ref, lq2_ref, lk2_ref, subg_ref, q_ref, k_ref, v_ref,
                  o_ref, vt_ref, bias_ref, m_ref, acc_ref, sa_ref, sb_ref, *, tq, hpb, lam_init, scale):
    hg = pl.program_id(1)
    qi = pl.program_id(2)
    seq = k_ref.shape[0]
    heads = range(hpb)
    lanes = [slice(e * LANES, (e + 1) * LANES) for e in heads]
    slope = [slopes_ref[hg * hpb + e] for e in heads]

    @pl.when(qi == 0)
    def _():
        for e in heads:
            for c in range(seq // tq):
                blk = v_ref[c * tq:(c + 1) * tq, lanes[e]].astype(F32)
                vt_ref[e, 0:LANES, c * tq:(c + 1) * tq] = blk.T.astype(BF16)
            vt_ref[e, LANES:, :] = jnp.ones((ONES_ROWS, seq), BF16)
            bias_ref[e] = slope[e] * lax.broadcasted_iota(jnp.int32, bias_ref.shape[1:], 0).astype(F32)

    qq = [_split_maps(q_ref[:, lanes[e]], scale) for e in heads]
    m_ref[...] = jnp.full(m_ref.shape, NEG, F32)
    acc_ref[...] = jnp.zeros(acc_ref.shape, F32)

    def scores(c, dst_ref):
        start = pl.multiple_of(c * tq, tq)
        for e in heads:
            dst_ref[e] = bias_ref[e] + _dot_nt(k_ref[pl.ds(start, tq), lanes[e]], qq[e])

    def chunk(c, src_ref, masked):
        start = pl.multiple_of(c * tq, tq)
        for e in heads:
            st = src_ref[e]
            if masked:
                kr = lax.broadcasted_iota(jnp.int32, st.shape, 0)
                qc = lax.broadcasted_iota(jnp.int32, st.shape, 1)
                qc = jnp.where(qc >= tq, qc - tq, qc)
                st = jnp.where(qc >= kr, st, NEG)
            m_prev = m_ref[e] - slope[e] * tq
            m_new = jnp.maximum(m_prev, jnp.max(st, axis=0, keepdims=True))
            alpha = jnp.exp(m_prev - m_new)
            p = jnp.exp(st - m_new).astype(BF16)
            acc_ref[e] = alpha * acc_ref[e] + _dot(vt_ref[e, :, pl.ds(start, tq)], p)
            m_ref[e] = m_new

    scores(0, sa_ref)

    def pair(i, carry):
        c = 2 * i
        scores(c + 1, sb_ref)
        chunk(c, sa_ref, False)
        scores(c + 2, sa_ref)
        chunk(c + 1, sb_ref, False)
        return carry

    lax.fori_loop(0, qi // 2, pair, 0)

    @pl.when(qi % 2 == 1)
    def _():
        scores(qi, sb_ref)
        chunk(qi - 1, sa_ref, False)
        chunk(qi, sb_ref, True)

    @pl.when(qi % 2 == 0)
    def _():
        chunk(qi, sa_ref, True)

    lam = _lambda_value(lq1_ref[...], lk1_ref[...], lq2_ref[...], lk2_ref[...], lam_init)
    for e in heads:
        acc = acc_ref[e]
        on = acc[:LANES] / acc[LANES:LANES + 1]
        a = (on[:, :tq] - lam * on[:, tq:]).T
        r = lax.rsqrt(jnp.mean(a * a, axis=-1, keepdims=True) + EPS)
        o_ref[:, lanes[e]] = (((a * r) * subg_ref[...]) * (1.0 - lam_init)).astype(o_ref.dtype)


def _prompt_attention(qkv, slopes, lam_vecs, subg, n_heads, lam_init):
    b, l, _ = qkv.shape
    tq = _tile(l, 256)
    hd = LANES
    hpb = 2 if n_heads % 2 == 0 else 1
    ng = n_heads // hpb
    vec = lambda n: pl.BlockSpec((1, n), lambda b_, h, i: (0, 0))
    kern = functools.partial(_pattn_kernel, tq=tq, hpb=hpb, lam_init=lam_init, scale=(hd // 2) ** -0.5)
    return pl.pallas_call(
        kern,
        grid=(b, ng, l // tq),
        in_specs=[pl.BlockSpec(memory_space=pltpu.SMEM),
                  vec(hd // 2), vec(hd // 2), vec(hd // 2), vec(hd // 2), vec(hd),
                  pl.BlockSpec((None, tq, hpb * hd), lambda b_, h, i: (b_, i, h)),
                  pl.BlockSpec((None, l, hpb * hd), lambda b_, h, i: (b_, 0, ng + h)),
                  pl.BlockSpec((None, l, hpb * hd), lambda b_, h, i: (b_, 0, 2 * ng + h))],
        out_specs=pl.BlockSpec((None, tq, hpb * hd), lambda b_, h, i: (b_, i, h)),
        out_shape=jax.ShapeDtypeStruct((b, l, n_heads * hd), BF16),
        scratch_shapes=[pltpu.VMEM((hpb, hd + ONES_ROWS, l), BF16), pltpu.VMEM((hpb, tq, 2 * tq), F32),
                        pltpu.VMEM((hpb, 1, 2 * tq), F32), pltpu.VMEM((hpb, hd + ONES_ROWS, 2 * tq), F32),
                        pltpu.VMEM((hpb, tq, 2 * tq), F32), pltpu.VMEM((hpb, tq, 2 * tq), F32)],
        compiler_params=_cparams(("arbitrary", "arbitrary", "arbitrary")),
        name="prompt_attn",
    )(slopes, *lam_vecs, subg, qkv, qkv, qkv)


PAGE_RING = 3


def _sattn_kernel(pt_ref, slope_ref, bias_ref, biasn_ref, lq1_ref, lk1_ref, lq2_ref, lk2_ref, subg_ref,
                  q_ref, kn_ref, vn_ref, ck_hbm, cv_hbm, o_ref, kbuf, vbuf, sem, m_ref, l_ref, acc_ref,
                  *, layer, n_pages_step, n_seq, ng, page, lam_init):
    b = pl.program_id(0)
    g = pl.program_id(1)
    n = b * ng + g
    total = n_seq * ng

    def group_copies(seq, grp, slot):
        copies = []
        for i in range(n_pages_step):
            pid = pt_ref[seq, grp * n_pages_step + i]
            copies.append(pltpu.make_async_copy(ck_hbm.at[layer, pid], kbuf.at[slot, i], sem.at[0, slot]))
            copies.append(pltpu.make_async_copy(cv_hbm.at[layer, pid], vbuf.at[slot, i], sem.at[1, slot]))
        return copies

    @pl.when(n == 0)
    def _():
        for d in range(min(PAGE_RING - 1, total)):
            for cp in group_copies(d // ng, d % ng, d):
                cp.start()

    nxt = n + (PAGE_RING - 1)

    @pl.when(nxt < total)
    def _():
        for cp in group_copies(lax.div(nxt, ng), lax.rem(nxt, ng), lax.rem(nxt, PAGE_RING)):
            cp.start()

    slot = lax.rem(n, PAGE_RING)
    for cp in group_copies(b, g, slot):
        cp.wait()
    k_refs = [kbuf.at[slot, i] for i in range(n_pages_step)]
    v_refs = [vbuf.at[slot, i] for i in range(n_pages_step)]

    @pl.when(g == 0)
    def _():
        m_ref[...] = jnp.full(m_ref.shape, NEG, F32)
        l_ref[...] = jnp.zeros(l_ref.shape, F32)
        acc_ref[...] = jnp.zeros(acc_ref.shape, F32)

    qq = q_ref[...]
    step = slope_ref[...] * (page * n_pages_step)

    def update(s, vs):
        m_prev = m_ref[...] - step
        m_new = jnp.maximum(m_prev, jnp.max(s, axis=-1, keepdims=True))
        alpha = jnp.exp(m_prev - m_new)
        p = jnp.exp(s - m_new)
        l_ref[...] = alpha * l_ref[...] + jnp.sum(p, axis=-1, keepdims=True)
        pb = p.astype(BF16)
        pv = None
        off = 0
        for vf in vs:
            part = _dot(pb[:, off:off + vf.shape[0]], vf)
            pv = part if pv is None else pv + part
            off += vf.shape[0]
        acc_ref[...] = alpha * acc_ref[...] + pv
        m_ref[...] = m_new

    s = jnp.concatenate([_dot_nt(qq, k_refs[i][...].reshape(-1, LANES).astype(BF16))
                         for i in range(n_pages_step)], axis=1) + bias_ref[...]
    update(s, [v_refs[i][...].reshape(-1, LANES).astype(BF16) for i in range(n_pages_step)])

    @pl.when(g == ng - 1)
    def _():
        update(_dot_nt(qq, kn_ref[...]) + biasn_ref[...], [vn_ref[...]])
        lam = _lambda_value(lq1_ref[...], lk1_ref[...], lq2_ref[...], lk2_ref[...], lam_init)
        half = acc_ref.shape[0] // 2
        o_ref[...] = _diff_finish(acc_ref[...], l_ref[...], lam, subg_ref[...], lam_init, half)


def _sample_attention(qrows, knew, vnew, cache_k, cache_v, layer, page_table, slopes, lam_vecs, subg, lam_init):
    db, nrow, hd = qrows.shape
    _, n_phys, page, n_heads, _ = cache_k.shape
    s_new = knew.shape[1] // n_heads
    n_pages = page_table.shape[1]
    pps = 8 if n_pages % 8 == 0 else 1
    r = jnp.arange(nrow)
    r_head = (r // s_new) % n_heads
    r_tok = r % s_new
    slope_rows = slopes[r_head][:, None]
    c = jnp.arange(pps * page * n_heads)
    bias = jnp.where(r_head[:, None] == (c % n_heads)[None, :], slope_rows * (c // n_heads)[None, :].astype(F32), NEG)
    cn = jnp.arange(s_new * n_heads)
    ok = (r_head[:, None] == (cn % n_heads)[None, :]) & ((cn // n_heads)[None, :] <= r_tok[:, None])
    biasn = jnp.where(ok, slope_rows * (cn // n_heads)[None, :].astype(F32), NEG)

    const = lambda shape: pl.BlockSpec(shape, lambda b, g, pt: (0,) * len(shape))
    kern = functools.partial(_sattn_kernel, layer=layer, n_pages_step=pps, n_seq=db, ng=n_pages // pps,
                             page=float(page), lam_init=lam_init)
    ring = (PAGE_RING, pps, page, n_heads, hd)
    grid_spec = pltpu.PrefetchScalarGridSpec(
        num_scalar_prefetch=1,
        grid=(db, n_pages // pps),
        in_specs=[const((nrow, 1)), const((nrow, pps * page * n_heads)), const((nrow, s_new * n_heads)),
                  const((1, hd // 2)), const((1, hd // 2)), const((1, hd // 2)), const((1, hd // 2)), const((1, hd)),
                  pl.BlockSpec((None, nrow, hd), lambda b, g, pt: (b, 0, 0)),
                  pl.BlockSpec((None, s_new * n_heads, hd), lambda b, g, pt: (b, 0, 0)),
                  pl.BlockSpec((None, s_new * n_heads, hd), lambda b, g, pt: (b, 0, 0)),
                  pl.BlockSpec(memory_space=pl.ANY), pl.BlockSpec(memory_space=pl.ANY)],
        out_specs=pl.BlockSpec((None, nrow // 2, hd), lambda b, g, pt: (b, 0, 0)),
        scratch_shapes=[pltpu.VMEM(ring, F32), pltpu.VMEM(ring, F32), pltpu.SemaphoreType.DMA((2, PAGE_RING)),
                        pltpu.VMEM((nrow, 1), F32), pltpu.VMEM((nrow, 1), F32), pltpu.VMEM((nrow, hd), F32)],
    )
    return pl.pallas_call(
        kern,
        grid_spec=grid_spec,
        out_shape=jax.ShapeDtypeStruct((db, nrow // 2, hd), F32),
        compiler_params=_cparams(("arbitrary", "arbitrary")),
        name="sample_attn",
    )(page_table, slope_rows, bias, biasn, *lam_vecs, subg, qrows, knew, vnew, cache_k, cache_v)


def _gdn_prep_kernel(x_ref, sm_ref, st_ref, cw_ref, alog_ref, dtb_ref, q_ref, k_ref, v_ref, gate_ref, pad_ref,
                     *, tt, n_heads, valid_len, taps):
    t = pl.program_id(1)
    halo = 8
    w3 = n_heads * LANES

    @pl.when(t == 0)
    def _():
        pad_ref[0:halo, :] = jnp.zeros((halo, pad_ref.shape[1]), F32)
        pad_ref[halo - (taps - 1):halo, :] = st_ref[...]

    pad_ref[halo:halo + tt, :] = x_ref[...]
    y = pad_ref[halo:halo + tt, :] * cw_ref[taps - 1:taps, :]
    for j in range(taps - 1):
        off = halo - (taps - 1) + j
        y = y + pad_ref[off:off + tt, :] * cw_ref[j:j + 1, :]
    y = _silu(y)
    pad_ref[0:halo, :] = pad_ref[tt:tt + halo, :]

    def l2n(z):
        return z * lax.rsqrt(jnp.sum(z * z, axis=-1, keepdims=True) + EPS)

    for hh in range(n_heads):
        sl = slice(hh * LANES, (hh + 1) * LANES)
        q_ref[:, sl] = l2n(y[:, sl]) * (LANES ** -0.5)
        k_ref[:, sl] = l2n(y[:, w3 + hh * LANES:w3 + (hh + 1) * LANES])
    v_ref[...] = y[:, 2 * w3:]

    x = sm_ref[...]
    lane = lax.broadcasted_iota(jnp.int32, x.shape, 1)
    row = lax.broadcasted_iota(jnp.int32, x.shape, 0) + t * tt
    beta = jax.nn.sigmoid(x)
    z = x + dtb_ref[...]
    softplus = jnp.maximum(z, 0.0) + jnp.log(1.0 + jnp.exp(-jnp.abs(z)))
    gdec = -jnp.exp(alog_ref[...]) * softplus
    out = jnp.where(lane < n_heads, beta, gdec)
    gate_ref[...] = jnp.where((row < valid_len) & (lane < 2 * n_heads), out, 0.0)


def _gdn_prep(x, small, state, conv_w, a_log, dt_bias, n_heads, valid_len):
    b, l, ch = x.shape
    taps = conv_w.shape[0]
    tt = _tile(l, 256)
    w = n_heads * LANES
    lanes = jnp.arange(LANES)
    in_g = (lanes >= n_heads) & (lanes < 2 * n_heads)
    alog_v = jnp.where(in_g, a_log[jnp.clip(lanes - n_heads, 0, n_heads - 1)], 0.0).reshape(1, LANES)
    dtb_v = jnp.where(in_g, dt_bias[jnp.clip(lanes - n_heads, 0, n_heads - 1)], 0.0).reshape(1, LANES)
    kern = functools.partial(_gdn_prep_kernel, tt=tt, n_heads=n_heads, valid_len=valid_len, taps=taps)
    row_spec = lambda n: pl.BlockSpec((None, tt, n), lambda b_, t: (b_, t, 0))
    return pl.pallas_call(
        kern,
        grid=(b, l // tt),
        in_specs=[row_spec(ch), row_spec(LANES),
                  pl.BlockSpec((None, taps - 1, ch), lambda b_, t: (b_, 0, 0)),
                  pl.BlockSpec((taps, ch), lambda b_, t: (0, 0)),
                  pl.BlockSpec((1, LANES), lambda b_, t: (0, 0)),
                  pl.BlockSpec((1, LANES), lambda b_, t: (0, 0))],
        out_specs=[row_spec(w), row_spec(w), row_spec(w), row_spec(LANES)],
        out_shape=[jax.ShapeDtypeStruct((b, l, w), F32)] * 3 + [jax.ShapeDtypeStruct((b, l, LANES), F32)],
        scratch_shapes=[pltpu.VMEM((tt + 8, ch), F32)],
        compiler_params=_cparams(("arbitrary", "arbitrary")),
        name="gdn_prep",
    )(x, small, state, conv_w, alog_v, dtb_v)


def _gdn_kernel(q_ref, k_ref, v_ref, gate_ref, z_ref, gn_ref, s0_ref, o_ref, s_ref, *, chunk, n_chunks, n_heads):
    t = pl.program_id(1)
    nb = q_ref.shape[0]

    @pl.when(t == 0)
    def _():
        s_ref[...] = s0_ref[...]

    c_ = chunk
    ri = lax.broadcasted_iota(jnp.int32, (c_, c_), 0)
    ci = lax.broadcasted_iota(jnp.int32, (c_, c_), 1)
    causal = ri >= ci
    strict = ri > ci
    tri = jnp.where(causal, 1.0, 0.0).astype(F32)
    eye = jnp.where(ri == ci, 1.0, 0.0).astype(F32)
    n_sq = max(int(math.log2(c_)) - 1, 0)
    gn = gn_ref[...]
    sel_r = lax.broadcasted_iota(jnp.int32, (8, LANES), 0)
    sel_c = lax.broadcasted_iota(jnp.int32, (8, LANES), 1)
    sel = jnp.where(sel_c == sel_r + n_heads, 1.0, 0.0).astype(F32)
    zpad = jnp.zeros((LANES - c_, LANES), F32)

    def one_chunk(c, carry):
        rows = pl.ds(c * c_ if n_chunks == 1 else pl.multiple_of(c * c_, c_), c_)
        gates = [gate_ref[bb, rows, :] for bb in range(nb)]
        gcum = [jnp.dot(tri, gates[bb], preferred_element_type=F32, precision=HI) for bb in range(nb)]
        gcum_t = [_dot_nt(sel, gcum[bb], precision=HI) for bb in range(nb)]
        hs = range(nb * n_heads)
        seq = [u_ // n_heads for u_ in hs]
        head = [u_ % n_heads for u_ in hs]
        sl = [(seq[u_], rows, slice(head[u_] * LANES, (head[u_] + 1) * LANES)) for u_ in hs]
        q = [q_ref[sl[hh]] for hh in hs]
        k = [k_ref[sl[hh]] for hh in hs]
        beta = [gates[seq[hh]][:, head[hh]:head[hh] + 1] for hh in hs]
        gc = [gcum[seq[hh]][:, n_heads + head[hh]:n_heads + head[hh] + 1] for hh in hs]
        glast = [gcum[seq[hh]][c_ - 1:c_, n_heads + head[hh]:n_heads + head[hh] + 1] for hh in hs]
        decay = [jnp.exp(jnp.where(causal, gc[hh] - gcum_t[seq[hh]][head[hh]:head[hh] + 1, :], NEG)) for hh in hs]
        kb = [k[hh].astype(BF16) for hh in hs]
        qk_kk = [_dot_nt(jnp.concatenate([kb[hh], q[hh].astype(BF16)], axis=0), kb[hh]) for hh in hs]
        a = [jnp.where(strict, beta[hh] * qk_kk[hh][:c_] * decay[hh], 0.0) for hh in hs]
        tinv = [eye - a[hh] for hh in hs]
        ap = a
        for _ in range(n_sq):
            ap = [_dot3(ap[hh], ap[hh]) for hh in hs]
            tinv = [tinv[hh] + _dot3(tinv[hh], ap[hh]) for hh in hs]
        eg = [jnp.exp(gc[hh]) for hh in hs]
        rhs = [jnp.concatenate([v_ref[sl[hh]] * beta[hh], k[hh] * (beta[hh] * eg[hh])], axis=1).astype(BF16)
               for hh in hs]
        uw = [_dot(tinv[hh].astype(BF16), rhs[hh]) for hh in hs]
        s = [s_ref[seq[hh], head[hh]] for hh in hs]
        wq = [_dot(jnp.concatenate([uw[hh][:, LANES:], q[hh] * eg[hh]], axis=0).astype(BF16), s[hh].astype(BF16))
              for hh in hs]
        u = [uw[hh][:, :LANES] - wq[hh][:c_] for hh in hs]
        ub = [u[hh].astype(BF16) for hh in hs]
        o = [wq[hh][c_:] + _dot((qk_kk[hh][c_:] * decay[hh]).astype(BF16), ub[hh]) for hh in hs]
        for hh in hs:
            kd = k[hh] * jnp.exp(glast[hh] - gc[hh])
            kd_t = jnp.concatenate([kd, zpad], axis=0).T.astype(BF16)
            u_pad = jnp.concatenate([u[hh], zpad], axis=0).astype(BF16)
            s_ref[seq[hh], head[hh]] = s[hh] * jnp.exp(glast[hh]) + _dot(kd_t, u_pad)
        for hh in hs:
            r = lax.rsqrt(jnp.mean(o[hh] * o[hh], axis=-1, keepdims=True) + EPS)
            o_ref[sl[hh]] = (((o[hh] * r) * gn) * _silu(z_ref[sl[hh]].astype(F32))).astype(o_ref.dtype)
        return carry

    if n_chunks == 1:
        one_chunk(0, 0)
    else:
        lax.fori_loop(0, n_chunks, one_chunk, 0)


def _gdn(q, k, v, gates, z_arr, z_block, gn, s0, chunk, n_heads):
    b, l, w = q.shape
    tt = _tile(l, max(chunk, 256), chunk)
    nb = 2 if b % 2 == 0 else 1
    kern = functools.partial(_gdn_kernel, chunk=chunk, n_chunks=tt // chunk, n_heads=n_heads)
    row_spec = lambda n: pl.BlockSpec((nb, tt, n), lambda b_, t: (b_, t, 0))
    st_spec = pl.BlockSpec((nb, n_heads, LANES, LANES), lambda b_, t: (b_, 0, 0, 0))
    return pl.pallas_call(
        kern,
        grid=(b // nb, l // tt),
        in_specs=[row_spec(w), row_spec(w), row_spec(w), row_spec(LANES),
                  pl.BlockSpec((nb, tt, w), lambda b_, t: (b_, t, z_block)),
                  pl.BlockSpec((1, LANES), lambda b_, t: (0, 0)),
                  st_spec],
        out_specs=[row_spec(w), st_spec],
        out_shape=[jax.ShapeDtypeStruct((b, l, w), BF16),
                   jax.ShapeDtypeStruct(s0.shape, F32)],
        compiler_params=_cparams(("arbitrary", "arbitrary")),
        name="gdn",
    )(q, k, v, gates, z_arr, gn, s0)


def _merge_kernel(a_ref, g_ref, ga_ref, gg_ref, wpa_ref, wpg_ref, wo_ref, x_ref, gate_ref, gn_ref, o_ref):
    pa = _dot(a_ref[...], wpa_ref[...])
    pg = _dot(g_ref[...], wpg_ref[...])
    m = jax.nn.sigmoid(ga_ref[...].astype(F32)) * pa + jax.nn.sigmoid(gg_ref[...].astype(F32)) * pg
    y = _dot(m.astype(BF16), wo_ref[...])
    r = lax.rsqrt(jnp.mean(y * y, axis=-1, keepdims=True) + EPS)
    o_ref[...] = x_ref[...] + gate_ref[...] * ((y * r) * gn_ref[...])


def _merge(a_out, g_out, zg, x, mod, rows_per_seq, per_row, w_pa, w_pg, w_o, gn):
    r, d = x.shape
    w = a_out.shape[1]
    tm = _tile(rows_per_seq if not per_row else r, 256)
    resident = lambda shape: pl.BlockSpec(shape, lambda i: (0, 0), pipeline_mode=pl.Buffered(1))
    return pl.pallas_call(
        _merge_kernel,
        grid=(r // tm,),
        in_specs=[pl.BlockSpec((tm, w), lambda i: (i, 0)),
                  pl.BlockSpec((tm, w), lambda i: (i, 0)),
                  pl.BlockSpec((tm, d), lambda i: (i, 0)),
                  pl.BlockSpec((tm, d), lambda i: (i, 1)),
                  resident((w, d)), resident((w, d)), resident((d, d)),
                  pl.BlockSpec((tm, d), lambda i: (i, 0)),
                  _mod_spec(rows_per_seq, tm, d, 2, per_row),
                  pl.BlockSpec((1, d), lambda i: (0, 0))],
        out_specs=pl.BlockSpec((tm, d), lambda i: (i, 0)),
        out_shape=jax.ShapeDtypeStruct((r, d), F32),
        compiler_params=_cparams(("arbitrary",)),
        name="merge",
    )(a_out, g_out, zg, zg, w_pa, w_pg, w_o, x, mod, gn)


def _ffn_up_kernel(x_ref, g_ref, sh_ref, sc_ref, wg_ref, wu_ref, o_ref, h_ref):
    @pl.when(pl.program_id(1) == 0)
    def _():
        h_ref[...] = _modulated(x_ref[...], g_ref[...], sh_ref[...], sc_ref[...]).astype(BF16)

    h = h_ref[...]
    gt = _dot(h, wg_ref[...])
    up = _dot(h, wu_ref[...])
    o_ref[...] = (_silu(gt) * up).astype(o_ref.dtype)


def _ffn_up(x, mod, rows_per_seq, per_row, g, w_up):
    r, d = x.shape
    f = w_up.shape[1] // 2
    tn = _tile(f, 512, LANES)
    nj = f // tn
    tm = _tile(rows_per_seq if not per_row else r, 1024)
    return pl.pallas_call(
        _ffn_up_kernel,
        grid=(r // tm, nj),
        in_specs=[pl.BlockSpec((tm, d), lambda i, j: (i, 0)),
                  pl.BlockSpec((1, d), lambda i, j: (0, 0)),
                  _mod_spec(rows_per_seq, tm, d, 3, per_row),
                  _mod_spec(rows_per_seq, tm, d, 4, per_row),
                  pl.BlockSpec((d, tn), lambda i, j: (0, j)),
                  pl.BlockSpec((d, tn), lambda i, j: (0, nj + j))],
        out_specs=pl.BlockSpec((tm, tn), lambda i, j: (i, j)),
        out_shape=jax.ShapeDtypeStruct((r, f), BF16),
        scratch_shapes=[pltpu.VMEM((tm, d), BF16)],
        compiler_params=_cparams(("arbitrary", "arbitrary")),
        name="ffn_up",
    )(x, g, mod, mod, w_up, w_up)


def _ffn_down_kernel(a_ref, w_ref, x_ref, gate_ref, gn_ref, o_ref):
    y = _dot(a_ref[...], w_ref[...])
    r = lax.rsqrt(jnp.mean(y * y, axis=-1, keepdims=True) + EPS)
    o_ref[...] = x_ref[...] + gate_ref[...] * ((y * r) * gn_ref[...])


def _ffn_down(act, x, mod, rows_per_seq, per_row, w_down, gn):
    r, d = x.shape
    f = act.shape[1]
    tm = _tile(rows_per_seq if not per_row else r, 256)
    return pl.pallas_call(
        _ffn_down_kernel,
        grid=(r // tm,),
        in_specs=[pl.BlockSpec((tm, f), lambda i: (i, 0)),
                  pl.BlockSpec((f, d), lambda i: (0, 0), pipeline_mode=pl.Buffered(1)),
                  pl.BlockSpec((tm, d), lambda i: (i, 0)),
                  _mod_spec(rows_per_seq, tm, d, 5, per_row),
                  pl.BlockSpec((1, d), lambda i: (0, 0))],
        out_specs=pl.BlockSpec((tm, d), lambda i: (i, 0)),
        out_shape=jax.ShapeDtypeStruct((r, d), F32),
        compiler_params=_cparams(("arbitrary",)),
        name="ffn_down",
    )(act, w_down, x, mod, gn)


def _group(x, mod, per_row, attn_fn, conv_state, s0, chunk, valid_len, pw, n_heads):
    b, l, d = x.shape
    r = b * l
    w = n_heads * LANES
    xf = x.reshape(r, d)
    oattn, ok, ov, ogdn, ozg, osm = _inproj(xf, mod, l, per_row, pw["g_mix_pre"], pw["w_all"], pw["w_gates"],
                                            pw["b_main"], pw["w_small"], pw["b_small"], w)
    a_out = attn_fn(oattn)
    q, k, v, gates = _gdn_prep(ogdn.reshape(b, l, 3 * w), osm.reshape(b, l, LANES), conv_state,
                               pw["conv_w"], pw["a_log"], pw["dt_bias"], n_heads, valid_len)
    g_out, s_new = _gdn(q, k, v, gates, ozg.reshape(b, l, -1), (2 * d) // w, pw["gdn_norm"], s0, chunk, n_heads)
    x1 = _merge(a_out, g_out.reshape(r, w), ozg, xf, mod, l, per_row, pw["w_pa"], pw["w_pg"], pw["w_o"],
                pw["g_mix_post"])
    act = _ffn_up(x1, mod, l, per_row, pw["g_ffn_pre"], pw["w_up"])
    x2 = _ffn_down(act, x1, mod, l, per_row, pw["w_down"], pw["g_ffn_post"])
    kv = (ok.reshape(b, l, n_heads, LANES), ov.reshape(b, l, n_heads, LANES))
    return x2.reshape(b, l, d), kv, ogdn, s_new


def kernel(x_prompt, x_sample, c_prompt, c_sample, cache_k, cache_v, page_table, state_ssm, state_conv, w_ada, b_ada, norm_mix_pre, norm_mix_post, norm_ffn_pre, norm_ffn_post, w_in, b_in, conv_w, lambda_q1, lambda_k1, lambda_q2, lambda_k2, attn_subln, a_log, dt_bias, gdn_norm, w_proj_attn, w_proj_gdn, w_out, w_ffn_up, w_ffn_down):
    bsz, seq, d = x_prompt.shape
    db, s_new, _ = x_sample.shape
    depth = w_in.shape[0]
    n_heads = cache_k.shape[3]
    hd = cache_k.shape[4]
    assert hd == LANES and state_ssm.shape[-1] == LANES and state_ssm.shape[-2] == LANES
    assert state_ssm.shape[2] == n_heads
    w = n_heads * hd
    taps = conv_w.shape[1]
    chunk_p = 64 if seq % 64 == 0 else seq
    slopes = 2.0 ** (-8.0 * jnp.arange(1, n_heads + 1, dtype=F32) / n_heads)
    assert (2 * d) % w == 0
    s_pad = -(-s_new // 8) * 8

    xp, xs = x_prompt, x_sample
    outs = [[] for _ in range(8)]
    for l in range(depth):
        lam_init = 0.8 - 0.6 * math.exp(-0.3 * l)
        lam_vecs = [t[l].reshape(1, -1) for t in (lambda_q1, lambda_k1, lambda_q2, lambda_k2)]
        subg = attn_subln[l].reshape(1, hd)

        c_all = jnp.concatenate([c_prompt, c_sample], axis=0)
        c_all = jnp.pad(c_all, ((0, (-c_all.shape[0]) % 8), (0, 0)))
        mod = _adaln(c_all, w_ada[l], b_ada[l])
        mod_p = mod[:bsz].reshape(bsz, 1, 6 * d)
        mod_s = jnp.repeat(mod[bsz:bsz + db], s_pad, axis=0)

        wt, bi = w_in[l].T, b_in[l]
        n6 = 6 * w
        n7 = 7 * w
        w_all = wt.astype(BF16)
        pw = dict(
            w_all=w_all, w_gates=w_all[n7 + 2 * n_heads:],
            b_main=jnp.concatenate([bi[:n6], bi[n7 + 2 * n_heads:], bi[n6:n7]]).reshape(1, -1),
            w_small=jnp.pad(w_all[n7:n7 + 2 * n_heads], ((0, LANES - 2 * n_heads), (0, 0))),
            b_small=jnp.pad(bi[n7:n7 + 2 * n_heads], (0, LANES - 2 * n_heads)).reshape(1, LANES),
            g_mix_pre=norm_mix_pre[l].reshape(1, d), g_mix_post=norm_mix_post[l].reshape(1, d),
            g_ffn_pre=norm_ffn_pre[l].reshape(1, d), g_ffn_post=norm_ffn_post[l].reshape(1, d),
            conv_w=conv_w[l], a_log=a_log[l], dt_bias=dt_bias[l], gdn_norm=gdn_norm[l].reshape(1, hd),
            w_pa=w_proj_attn[l].astype(BF16), w_pg=w_proj_gdn[l].astype(BF16), w_o=w_out[l].astype(BF16),
            w_up=w_ffn_up[l].astype(BF16), w_down=w_ffn_down[l].astype(BF16),
        )

        def attn_p(oattn):
            o = _prompt_attention(oattn.reshape(bsz, seq, 3 * w), slopes, lam_vecs, subg, n_heads, lam_init)
            return o.reshape(bsz * seq, w)

        conv0 = jnp.zeros((bsz, taps - 1, 3 * w), F32)
        s0 = jnp.zeros((bsz, n_heads, hd, hd), F32)
        xp, kv_p, raw_p, ssm_p = _group(xp, mod_p, False, attn_p, conv0, s0, chunk_p, seq, pw, n_heads)

        xs_pad = jnp.pad(xs, ((0, 0), (0, s_pad - s_new), (0, 0)))

        def attn_s(oattn):
            qkv = oattn.reshape(db, s_pad, 3, n_heads, hd)[:, :s_new]
            qh = qkv[:, :, 0].transpose(0, 2, 1, 3)
            qrows = jax.vmap(lambda t: _split_maps(t, (hd // 2) ** -0.5))(qh.reshape(db, n_heads * s_new, hd))
            knew = qkv[:, :, 1].reshape(db, s_new * n_heads, hd)
            vnew = qkv[:, :, 2].reshape(db, s_new * n_heads, hd)
            o = _sample_attention(qrows, knew, vnew, cache_k, cache_v, l, page_table, slopes, lam_vecs, subg,
                                  lam_init)
            o = o.reshape(db, n_heads, s_new, hd).transpose(0, 2, 1, 3).reshape(db, s_new, w)
            return jnp.pad(o, ((0, 0), (0, s_pad - s_new), (0, 0))).reshape(db * s_pad, w).astype(BF16)

        xs_full, kv_s, raw_s, ssm_s = _group(xs_pad, mod_s, True, attn_s, state_conv[l], state_ssm[l], s_pad,
                                             s_new, pw, n_heads)
        xs = xs_full[:, :s_new]

        raw_p = raw_p.reshape(bsz, seq, 3 * w)
        raw_s = jnp.concatenate([state_conv[l], raw_s.reshape(db, s_pad, 3 * w)[:, :s_new]], axis=1)
        for lst, val in zip(outs, (kv_p[0], kv_p[1], kv_s[0][:, :s_new], kv_s[1][:, :s_new], ssm_p, ssm_s,
                                   raw_p[:, seq - (taps - 1):], raw_s[:, s_new:])):
            lst.append(val)

    return (xp, xs) + tuple(jnp.stack(o) for o in outs)
```

```python
import functools
import math

import jax
import jax.numpy as jnp
from jax import lax
from jax.experimental import pallas as pl
from jax.experimental.pallas import tpu as pltpu

F32 = jnp.float32
BF16 = jnp.bfloat16
EPS = 1e-6
NEG = -1e30
LANES = 128
VMEM_LIMIT = 56 * 1024 * 1024
HI = lax.Precision.HIGHEST
LOG2E = math.log2(math.e)


def _cparams(sem):
    return pltpu.CompilerParams(dimension_semantics=sem, vmem_limit_bytes=VMEM_LIMIT)


def _tile(n, target, mult=8):
    if n <= target:
        return n
    for t in range(target, 0, -1):
        if n % t == 0 and t % mult == 0:
            return t
    return n


def _silu(x):
    return x * jax.nn.sigmoid(x)


def _dot(a, b):
    return jnp.dot(a, b, preferred_element_type=F32)


def _split(a):
    hi = a.astype(BF16)
    return hi, (a - hi.astype(F32)).astype(BF16)


def _dot3(a, b):
    ah, al = _split(a)
    bh, bl = _split(b)
    return _dot(ah, bh) + (_dot(ah, bl) + _dot(al, bh))


def _dot_nt(a, b, precision=None):
    return lax.dot_general(a, b, (((1,), (1,)), ((), ())), preferred_element_type=F32,
                           precision=precision)


def _dot_tn(a, b):
    return lax.dot_general(a, b, (((0,), (0,)), ((), ())), preferred_element_type=F32)


def _adaln_kernel(c_ref, w_ref, b_ref, o_ref):
    a = _silu(c_ref[...]).astype(BF16)
    o_ref[...] = _dot(a, w_ref[...].astype(BF16)) + b_ref[...]


def _adaln(c, w, b):
    m, d = c.shape
    n = w.shape[1]
    tn = _tile(n, 1024, LANES)
    return pl.pallas_call(
        _adaln_kernel,
        grid=(n // tn,),
        in_specs=[pl.BlockSpec((m, d), lambda j: (0, 0)),
                  pl.BlockSpec((d, tn), lambda j: (0, j)),
                  pl.BlockSpec((1, tn), lambda j: (0, j))],
        out_specs=pl.BlockSpec((m, tn), lambda j: (0, j)),
        out_shape=jax.ShapeDtypeStruct((m, n), F32),
        compiler_params=_cparams(("arbitrary",)),
        name="adaln",
    )(c, w, b.reshape(1, n))


def _modulated(x, g, shift, scale):
    r = lax.rsqrt(jnp.mean(x * x, axis=-1, keepdims=True) + EPS)
    return ((x * r) * g) * (1.0 + scale) + shift


def _inproj_kernel(x_ref, g_ref, sh_ref, sc_ref, wa_ref, wg_ref, b_ref, ws_ref, bs_ref,
                   oattn_ref, ok_ref, ov_ref, ogdn_ref, ozg_ref, osm_ref, h_ref, *, spw):
    j = pl.program_id(1)
    nj = pl.num_programs(1)
    hps = ok_ref.shape[1] // spw

    @pl.when(j == 0)
    def _():
        hb = _modulated(x_ref[...], g_ref[...], sh_ref[...], sc_ref[...]).astype(BF16)
        h_ref[...] = hb
        osm_ref[...] = _dot_nt(hb, ws_ref[...]) + bs_ref[...]

    def project(w_ref):
        return _dot_nt(h_ref[...], w_ref[...]) + b_ref[...]

    def heads_to_sublanes(y, o_ref, first):
        o_ref[:, first:first + hps, :] = pltpu.einshape("t(hd)->thd", y, h=hps)

    @pl.when(j < 3 * spw)
    def _():
        y = project(wa_ref)
        oattn_ref[...] = y.astype(BF16)
        for part in range(spw):
            @pl.when(j == spw + part)
            def _():
                heads_to_sublanes(y, ok_ref, part * hps)

            @pl.when(j == 2 * spw + part)
            def _():
                heads_to_sublanes(y, ov_ref, part * hps)

    @pl.when((j >= 3 * spw) & (j < 6 * spw))
    def _():
        ogdn_ref[...] = project(wa_ref)

    @pl.when((j >= 6 * spw) & (j < nj - spw))
    def _():
        ozg_ref[...] = project(wg_ref).astype(BF16)

    @pl.when(j >= nj - spw)
    def _():
        ozg_ref[...] = project(wa_ref).astype(BF16)


def _mod_spec(rows_per_seq, tm, d, chunk, per_row):
    if per_row:
        return pl.BlockSpec((tm, d), lambda i, *_: (i, chunk))
    return pl.BlockSpec((None, 1, d), lambda i, *_: ((i * tm) // rows_per_seq, 0, chunk))


def _inproj(x, mod, rows_per_seq, per_row, g, w_all, w_gates, b_main, w_small, b_small, width):
    r, d = x.shape
    n_heads = width // LANES
    tm = _tile(rows_per_seq if not per_row else r, 1024)
    spw = 2 if (n_heads % 2 == 0 and tm > 512) else 1
    tn = width // spw
    n_gate = w_gates.shape[0] // tn
    nj = 7 * spw + n_gate
    grid = (r // tm, nj)
    clip = lambda j, lo, hi: jnp.minimum(jnp.maximum(j - lo, 0), hi - lo)
    return pl.pallas_call(
        functools.partial(_inproj_kernel, spw=spw),
        grid=grid,
        in_specs=[pl.BlockSpec((tm, d), lambda i, j: (i, 0), pipeline_mode=pl.Buffered(1)),
                  pl.BlockSpec((1, d), lambda i, j: (0, 0)),
                  _mod_spec(rows_per_seq, tm, d, 0, per_row),
                  _mod_spec(rows_per_seq, tm, d, 1, per_row),
                  pl.BlockSpec((tn, d), lambda i, j: (jnp.where(j < 6 * spw, j, clip(j, nj - spw, nj - 1) + 6 * spw), 0)),
                  pl.BlockSpec((tn, d), lambda i, j: (clip(j, 6 * spw, 6 * spw + n_gate - 1), 0)),
                  pl.BlockSpec((1, tn), lambda i, j: (0, j)),
                  pl.BlockSpec((LANES, d), lambda i, j: (0, 0)),
                  pl.BlockSpec((1, LANES), lambda i, j: (0, 0))],
        out_specs=[pl.BlockSpec((tm, tn), lambda i, j: (i, clip(j, 0, 3 * spw - 1))),
                   pl.BlockSpec((tm, n_heads, LANES), lambda i, j: (i, 0, 0)),
                   pl.BlockSpec((tm, n_heads, LANES), lambda i, j: (i, 0, 0)),
                   pl.BlockSpec((tm, tn), lambda i, j: (i, clip(j, 3 * spw, 6 * spw - 1))),
                   pl.BlockSpec((tm, tn), lambda i, j: (i, clip(j, 6 * spw, nj - 1))),
                   pl.BlockSpec((tm, LANES), lambda i, j: (i, 0))],
        out_shape=[jax.ShapeDtypeStruct((r, 3 * width), BF16),
                   jax.ShapeDtypeStruct((r, n_heads, LANES), F32),
                   jax.ShapeDtypeStruct((r, n_heads, LANES), F32),
                   jax.ShapeDtypeStruct((r, 3 * width), F32),
                   jax.ShapeDtypeStruct((r, (nj - 6 * spw) * tn), BF16),
                   jax.ShapeDtypeStruct((r, LANES), F32)],
        scratch_shapes=[pltpu.VMEM((tm, d), BF16)],
        compiler_params=_cparams(("arbitrary", "arbitrary")),
        name="inproj",
    )(x, g, mod, mod, w_all, w_gates, b_main, w_small, b_small)


def _lambda_value(lq1, lk1, lq2, lk2, lam_init):
    s1 = jnp.sum(lq1 * lk1, axis=-1, keepdims=True)
    s2 = jnp.sum(lq2 * lk2, axis=-1, keepdims=True)
    return jnp.exp(s1) - jnp.exp(s2) + lam_init


def _diff_finish(acc, l, lam, subg, lam_init, half):
    o1 = acc[:half] / l[:half]
    o2 = acc[half:] / l[half:]
    a = o1 - lam * o2
    r = lax.rsqrt(jnp.mean(a * a, axis=-1, keepdims=True) + EPS)
    return ((a * r) * subg) * (1.0 - lam_init)


def _split_maps(q, scale):
    lane = lax.broadcasted_iota(jnp.int32, q.shape, 1)
    qs = (q.astype(F32) * scale).astype(q.dtype)
    zero = jnp.zeros_like(qs)
    half = q.shape[-1] // 2
    return jnp.concatenate([jnp.where(lane < half, qs, zero), jnp.where(lane >= half, qs, zero)], axis=0)


ONES_ROWS = 16


def _pattn_kernel(slopes_ref, lq1_ref, lk1_ref, lq2_ref, lk2_ref, subg_ref, q_ref, k_ref, v_ref,
                  o_ref, vt_ref, bias_ref, m_ref, acc_ref, sa_ref, sb_ref, *, tq, hpb, lam_init, scale):
    hg = pl.program_id(1)
    qi = pl.program_id(2)
    seq = k_ref.shape[0]
    heads = range(hpb)
    lanes = [slice(e * LANES, (e + 1) * LANES) for e in heads]
    slope = [slopes_ref[hg * hpb + e] * LOG2E for e in heads]

    @pl.when(qi == 0)
    def _():
        for e in heads:
            for c in range(seq // tq):
                blk = v_ref[c * tq:(c + 1) * tq, lanes[e]].astype(F32)
                vt_ref[e, 0:LANES, c * tq:(c + 1) * tq] = blk.T.astype(BF16)
            vt_ref[e, LANES:, :] = jnp.ones((ONES_ROWS, seq), BF16)
            bias_ref[e] = slope[e] * lax.broadcasted_iota(jnp.int32, bias_ref.shape[1:], 0).astype(F32)

    qq = [_split_maps(q_ref[:, lanes[e]], scale) for e in heads]
    m_ref[...] = jnp.full(m_ref.shape, NEG, F32)
    acc_ref[...] = jnp.zeros(acc_ref.shape, F32)

    def scores(c, dst_ref):
        start = pl.multiple_of(c * tq, tq)
        for e in heads:
            dst_ref[e] = _dot_nt(k_ref[pl.ds(start, tq), lanes[e]], qq[e]) + bias_ref[e]

    def chunk(c, src_ref, masked):
        start = pl.multiple_of(c * tq, tq)
        for e in heads:
            st = src_ref[e]
            if masked:
                kr = lax.broadcasted_iota(jnp.int32, st.shape, 0)
                qc = lax.broadcasted_iota(jnp.int32, st.shape, 1)
                qc = jnp.where(qc >= tq, qc - tq, qc)
                st = jnp.where(qc >= kr, st, NEG)
            m_prev = m_ref[e] - slope[e] * tq
            m_new = jnp.maximum(m_prev, jnp.max(st, axis=0, keepdims=True))
            alpha = jnp.exp2(m_prev - m_new)
            p = jnp.exp2(st - m_new).astype(BF16)
            acc_ref[e] = _dot(vt_ref[e, :, pl.ds(start, tq)], p) + alpha * acc_ref[e]
            m_ref[e] = m_new

    scores(0, sa_ref)

    def pair(i, carry):
        c = 2 * i
        scores(c + 1, sb_ref)
        chunk(c, sa_ref, False)
        scores(c + 2, sa_ref)
        chunk(c + 1, sb_ref, False)
        return carry

    lax.fori_loop(0, qi // 2, pair, 0)

    @pl.when(qi % 2 == 1)
    def _():
        scores(qi, sb_ref)
        chunk(qi - 1, sa_ref, False)
        chunk(qi, sb_ref, True)

    @pl.when(qi % 2 == 0)
    def _():
        chunk(qi, sa_ref, True)

    lam = _lambda_value(lq1_ref[...], lk1_ref[...], lq2_ref[...], lk2_ref[...], lam_init)
    for e in heads:
        acc = acc_ref[e]
        on = acc[:LANES] / acc[LANES:LANES + 1]
        a = (on[:, :tq] - lam * on[:, tq:]).T
        r = lax.rsqrt(jnp.mean(a * a, axis=-1, keepdims=True) + EPS)
        o_ref[:, lanes[e]] = (((a * r) * subg_ref[...]) * (1.0 - lam_init)).astype(o_ref.dtype)


def _prompt_attention(qkv, slopes, lam_vecs, subg, n_heads, lam_init):
    b, l, _ = qkv.shape
    tq = _tile(l, 256)
    hd = LANES
    hpb = 2 if n_heads % 2 == 0 else 1
    ng = n_heads // hpb
    vec = lambda n: pl.BlockSpec((1, n), lambda b_, h, i: (0, 0))
    kern = functools.partial(_pattn_kernel, tq=tq, hpb=hpb, lam_init=lam_init, scale=(hd // 2) ** -0.5 * LOG2E)
    return pl.pallas_call(
        kern,
        grid=(b, ng, l // tq),
        in_specs=[pl.BlockSpec(memory_space=pltpu.SMEM),
                  vec(hd // 2), vec(hd // 2), vec(hd // 2), vec(hd // 2), vec(hd),
                  pl.BlockSpec((None, tq, hpb * hd), lambda b_, h, i: (b_, i, h)),
                  pl.BlockSpec((None, l, hpb * hd), lambda b_, h, i: (b_, 0, ng + h)),
                  pl.BlockSpec((None, l, hpb * hd), lambda b_, h, i: (b_, 0, 2 * ng + h))],
        out_specs=pl.BlockSpec((None, tq, hpb * hd), lambda b_, h, i: (b_, i, h)),
        out_shape=jax.ShapeDtypeStruct((b, l, n_heads * hd), BF16),
        scratch_shapes=[pltpu.VMEM((hpb, hd + ONES_ROWS, l), BF16), pltpu.VMEM((hpb, tq, 2 * tq), F32),
                        pltpu.VMEM((hpb, 1, 2 * tq), F32), pltpu.VMEM((hpb, hd + ONES_ROWS, 2 * tq), F32),
                        pltpu.VMEM((hpb, tq, 2 * tq), F32), pltpu.VMEM((hpb, tq, 2 * tq), F32)],
        compiler_params=_cparams(("arbitrary", "arbitrary", "arbitrary")),
        name="prompt_attn",
    )(slopes, *lam_vecs, subg, qkv, qkv, qkv)


PAGE_RING = 3


def _sattn_kernel(pt_ref, slope_ref, bias_ref, biasn_ref, lq1_ref, lk1_ref, lq2_ref, lk2_ref, subg_ref,
                  q_ref, kn_ref, vn_ref, ck_hbm, cv_hbm, o_ref, kbuf, vbuf, sem, m_ref, l_ref, acc_ref,
                  *, layer, n_pages_step, n_seq, ng, page, lam_init):
    b = pl.program_id(0)
    g = pl.program_id(1)
    n = b * ng + g
    total = n_seq * ng

    def group_copies(seq, grp, slot):
        copies = []
        for i in range(n_pages_step):
            pid = pt_ref[seq, grp * n_pages_step + i]
            copies.append(pltpu.make_async_copy(ck_hbm.at[layer, pid], kbuf.at[slot, i], sem.at[0, slot]))
            copies.append(pltpu.make_async_copy(cv_hbm.at[layer, pid], vbuf.at[slot, i], sem.at[1, slot]))
        return copies

    @pl.when(n == 0)
    def _():
        for d in range(min(PAGE_RING - 1, total)):
            for cp in group_copies(d // ng, d % ng, d):
                cp.start()

    nxt = n + (PAGE_RING - 1)

    @pl.when(nxt < total)
    def _():
        for cp in group_copies(lax.div(nxt, ng), lax.rem(nxt, ng), lax.rem(nxt, PAGE_RING)):
            cp.start()

    slot = lax.rem(n, PAGE_RING)
    for cp in group_copies(b, g, slot):
        cp.wait()
    k_refs = [kbuf.at[slot, i] for i in range(n_pages_step)]
    v_refs = [vbuf.at[slot, i] for i in range(n_pages_step)]

    @pl.when(g == 0)
    def _():
        m_ref[...] = jnp.full(m_ref.shape, NEG, F32)
        l_ref[...] = jnp.zeros(l_ref.shape, F32)
        acc_ref[...] = jnp.zeros(acc_ref.shape, F32)

    qq = q_ref[...]
    step = slope_ref[...] * (page * n_pages_step)

    def update(s, vs):
        m_prev = m_ref[...] - step
        m_new = jnp.maximum(m_prev, jnp.max(s, axis=-1, keepdims=True))
        alpha = jnp.exp(m_prev - m_new)
        p = jnp.exp(s - m_new)
        l_ref[...] = alpha * l_ref[...] + jnp.sum(p, axis=-1, keepdims=True)
        pb = p.astype(BF16)
        pv = None
        off = 0
        for vf in vs:
            part = _dot(pb[:, off:off + vf.shape[0]], vf)
            pv = part if pv is None else pv + part
            off += vf.shape[0]
        acc_ref[...] = alpha * acc_ref[...] + pv
        m_ref[...] = m_new

    s = jnp.concatenate([_dot_nt(qq, k_refs[i][...].reshape(-1, LANES).astype(BF16))
                         for i in range(n_pages_step)], axis=1) + bias_ref[...]
    update(s, [v_refs[i][...].reshape(-1, LANES).astype(BF16) for i in range(n_pages_step)])

    @pl.when(g == ng - 1)
    def _():
        update(_dot_nt(qq, kn_ref[...]) + biasn_ref[...], [vn_ref[...]])
        lam = _lambda_value(lq1_ref[...], lk1_ref[...], lq2_ref[...], lk2_ref[...], lam_init)
        half = acc_ref.shape[0] // 2
        o_ref[...] = _diff_finish(acc_ref[...], l_ref[...], lam, subg_ref[...], lam_init, half)


def _sample_attention(qrows, knew, vnew, cache_k, cache_v, layer, page_table, slopes, lam_vecs, subg, lam_init):
    db, nrow, hd = qrows.shape
    _, n_phys, page, n_heads, _ = cache_k.shape
    s_new = knew.shape[1] // n_heads
    n_pages = page_table.shape[1]
    pps = 8 if n_pages % 8 == 0 else 1
    r = jnp.arange(nrow)
    r_head = (r // s_new) % n_heads
    r_tok = r % s_new
    slope_rows = slopes[r_head][:, None]
    c = jnp.arange(pps * page * n_heads)
    bias = jnp.where(r_head[:, None] == (c % n_heads)[None, :], slope_rows * (c // n_heads)[None, :].astype(F32), NEG)
    cn = jnp.arange(s_new * n_heads)
    ok = (r_head[:, None] == (cn % n_heads)[None, :]) & ((cn // n_heads)[None, :] <= r_tok[:, None])
    biasn = jnp.where(ok, slope_rows * (cn // n_heads)[None, :].astype(F32), NEG)

    const = lambda shape: pl.BlockSpec(shape, lambda b, g, pt: (0,) * len(shape))
    kern = functools.partial(_sattn_kernel, layer=layer, n_pages_step=pps, n_seq=db, ng=n_pages // pps,
                             page=float(page), lam_init=lam_init)
    ring = (PAGE_RING, pps, page, n_heads, hd)
    grid_spec = pltpu.PrefetchScalarGridSpec(
        num_scalar_prefetch=1,
        grid=(db, n_pages // pps),
        in_specs=[const((nrow, 1)), const((nrow, pps * page * n_heads)), const((nrow, s_new * n_heads)),
                  const((1, hd // 2)), const((1, hd // 2)), const((1, hd // 2)), const((1, hd // 2)), const((1, hd)),
                  pl.BlockSpec((None, nrow, hd), lambda b, g, pt: (b, 0, 0)),
                  pl.BlockSpec((None, s_new * n_heads, hd), lambda b, g, pt: (b, 0, 0)),
                  pl.BlockSpec((None, s_new * n_heads, hd), lambda b, g, pt: (b, 0, 0)),
                  pl.BlockSpec(memory_space=pl.ANY), pl.BlockSpec(memory_space=pl.ANY)],
        out_specs=pl.BlockSpec((None, nrow // 2, hd), lambda b, g, pt: (b, 0, 0)),
        scratch_shapes=[pltpu.VMEM(ring, F32), pltpu.VMEM(ring, F32), pltpu.SemaphoreType.DMA((2, PAGE_RING)),
                        pltpu.VMEM((nrow, 1), F32), pltpu.VMEM((nrow, 1), F32), pltpu.VMEM((nrow, hd), F32)],
    )
    return pl.pallas_call(
        kern,
        grid_spec=grid_spec,
        out_shape=jax.ShapeDtypeStruct((db, nrow // 2, hd), F32),
        compiler_params=_cparams(("arbitrary", "arbitrary")),
        name="sample_attn",
    )(page_table, slope_rows, bias, biasn, *lam_vecs, subg, qrows, knew, vnew, cache_k, cache_v)


def _gdn_prep_kernel(x_ref, sm_ref, st_ref, cw_ref, alog_ref, dtb_ref, q_ref, k_ref, v_ref, gate_ref, pad_ref,
                     *, tt, n_heads, valid_len, taps):
    t = pl.program_id(1)
    halo = 8
    w3 = n_heads * LANES

    @pl.when(t == 0)
    def _():
        pad_ref[0:halo, :] = jnp.zeros((halo, pad_ref.shape[1]), F32)
        pad_ref[halo - (taps - 1):halo, :] = st_ref[...]

    pad_ref[halo:halo + tt, :] = x_ref[...]
    y = pad_ref[halo:halo + tt, :] * cw_ref[taps - 1:taps, :]
    for j in range(taps - 1):
        off = halo - (taps - 1) + j
        y = y + pad_ref[off:off + tt, :] * cw_ref[j:j + 1, :]
    y = _silu(y)
    pad_ref[0:halo, :] = pad_ref[tt:tt + halo, :]

    def l2n(z):
        return z * lax.rsqrt(jnp.sum(z * z, axis=-1, keepdims=True) + EPS)

    for hh in range(n_heads):
        sl = slice(hh * LANES, (hh + 1) * LANES)
        q_ref[:, sl] = l2n(y[:, sl]) * (LANES ** -0.5)
        k_ref[:, sl] = l2n(y[:, w3 + hh * LANES:w3 + (hh + 1) * LANES])
    v_ref[...] = y[:, 2 * w3:]

    x = sm_ref[...]
    lane = lax.broadcasted_iota(jnp.int32, x.shape, 1)
    row = lax.broadcasted_iota(jnp.int32, x.shape, 0) + t * tt
    beta = jax.nn.sigmoid(x)
    z = x + dtb_ref[...]
    softplus = jnp.maximum(z, 0.0) + jnp.log(1.0 + jnp.exp(-jnp.abs(z)))
    gdec = -jnp.exp(alog_ref[...]) * softplus
    out = jnp.where(lane < n_heads, beta, gdec)
    gate_ref[...] = jnp.where((row < valid_len) & (lane < 2 * n_heads), out, 0.0)


def _gdn_prep(x, small, state, conv_w, a_log, dt_bias, n_heads, valid_len):
    b, l, ch = x.shape
    taps = conv_w.shape[0]
    tt = _tile(l, 256)
    w = n_heads * LANES
    lanes = jnp.arange(LANES)
    in_g = (lanes >= n_heads) & (lanes < 2 * n_heads)
    alog_v = jnp.where(in_g, a_log[jnp.clip(lanes - n_heads, 0, n_heads - 1)], 0.0).reshape(1, LANES)
    dtb_v = jnp.where(in_g, dt_bias[jnp.clip(lanes - n_heads, 0, n_heads - 1)], 0.0).reshape(1, LANES)
    kern = functools.partial(_gdn_prep_kernel, tt=tt, n_heads=n_heads, valid_len=valid_len, taps=taps)
    row_spec = lambda n: pl.BlockSpec((None, tt, n), lambda b_, t: (b_, t, 0))
    return pl.pallas_call(
        kern,
        grid=(b, l // tt),
        in_specs=[row_spec(ch), row_spec(LANES),
                  pl.BlockSpec((None, taps - 1, ch), lambda b_, t: (b_, 0, 0)),
                  pl.BlockSpec((taps, ch), lambda b_, t: (0, 0)),
                  pl.BlockSpec((1, LANES), lambda b_, t: (0, 0)),
                  pl.BlockSpec((1, LANES), lambda b_, t: (0, 0))],
        out_specs=[row_spec(w), row_spec(w), row_spec(w), row_spec(LANES)],
        out_shape=[jax.ShapeDtypeStruct((b, l, w), F32)] * 3 + [jax.ShapeDtypeStruct((b, l, LANES), F32)],
        scratch_shapes=[pltpu.VMEM((tt + 8, ch), F32)],
        compiler_params=_cparams(("arbitrary", "arbitrary")),
        name="gdn_prep",
    )(x, small, state, conv_w, alog_v, dtb_v)


def _gdn_kernel(q_ref, k_ref, v_ref, gate_ref, z_ref, gn_ref, s0_ref, o_ref, s_ref, *, chunk, n_chunks, n_heads):
    t = pl.program_id(1)
    nb = q_ref.shape[0]

    @pl.when(t == 0)
    def _():
        s_ref[...] = s0_ref[...]

    c_ = chunk
    ri = lax.broadcasted_iota(jnp.int32, (c_, c_), 0)
    ci = lax.broadcasted_iota(jnp.int32, (c_, c_), 1)
    causal = ri >= ci
    strict = ri > ci
    tri = jnp.where(causal, 1.0, 0.0).astype(F32)
    eye = jnp.where(ri == ci, 1.0, 0.0).astype(F32)
    n_sq = max(int(math.log2(c_)) - 1, 0)
    gn = gn_ref[...]
    sel_r = lax.broadcasted_iota(jnp.int32, (8, LANES), 0)
    sel_c = lax.broadcasted_iota(jnp.int32, (8, LANES), 1)
    sel = jnp.where(sel_c == sel_r + n_heads, 1.0, 0.0).astype(F32)
    zpad = jnp.zeros((LANES - c_, LANES), F32)

    def one_chunk(c, carry):
        rows = pl.ds(c * c_ if n_chunks == 1 else pl.multiple_of(c * c_, c_), c_)
        gates = [gate_ref[bb, rows, :] for bb in range(nb)]
        gcum = [jnp.dot(tri, gates[bb], preferred_element_type=F32, precision=HI) for bb in range(nb)]
        gcum_t = [_dot_nt(sel, gcum[bb], precision=HI) for bb in range(nb)]
        hs = range(nb * n_heads)
        seq = [u_ // n_heads for u_ in hs]
        head = [u_ % n_heads for u_ in hs]
        sl = [(seq[u_], rows, slice(head[u_] * LANES, (head[u_] + 1) * LANES)) for u_ in hs]
        q = [q_ref[sl[hh]] for hh in hs]
        k = [k_ref[sl[hh]] for hh in hs]
        beta = [gates[seq[hh]][:, head[hh]:head[hh] + 1] for hh in hs]
        gc = [gcum[seq[hh]][:, n_heads + head[hh]:n_heads + head[hh] + 1] for hh in hs]
        glast = [gcum[seq[hh]][c_ - 1:c_, n_heads + head[hh]:n_heads + head[hh] + 1] for hh in hs]
        decay = [jnp.exp(jnp.where(causal, gc[hh] - gcum_t[seq[hh]][head[hh]:head[hh] + 1, :], NEG)) for hh in hs]
        kb = [k[hh].astype(BF16) for hh in hs]
        qk_kk = [_dot_nt(jnp.concatenate([kb[hh], q[hh].astype(BF16)], axis=0), kb[hh]) for hh in hs]
        a = [jnp.where(strict, beta[hh] * qk_kk[hh][:c_] * decay[hh], 0.0) for hh in hs]
        tinv = [eye - a[hh] for hh in hs]
        ap = a
        for _ in range(n_sq):
            ap = [_dot3(ap[hh], ap[hh]) for hh in hs]
            tinv = [tinv[hh] + _dot3(tinv[hh], ap[hh]) for hh in hs]
        eg = [jnp.exp(gc[hh]) for hh in hs]
        rhs = [jnp.concatenate([v_ref[sl[hh]] * beta[hh], k[hh] * (beta[hh] * eg[hh])], axis=1).astype(BF16)
               for hh in hs]
        uw = [_dot(tinv[hh].astype(BF16), rhs[hh]) for hh in hs]
        s = [s_ref[seq[hh], head[hh]] for hh in hs]
        wq = [_dot(jnp.concatenate([uw[hh][:, LANES:], q[hh] * eg[hh]], axis=0).astype(BF16), s[hh].astype(BF16))
              for hh in hs]
        u = [uw[hh][:, :LANES] - wq[hh][:c_] for hh in hs]
        ub = [u[hh].astype(BF16) for hh in hs]
        o = [wq[hh][c_:] + _dot((qk_kk[hh][c_:] * decay[hh]).astype(BF16), ub[hh]) for hh in hs]
        for hh in hs:
            kd = k[hh] * jnp.exp(glast[hh] - gc[hh])
            kd_t = jnp.concatenate([kd, zpad], axis=0).T.astype(BF16)
            u_pad = jnp.concatenate([u[hh], zpad], axis=0).astype(BF16)
            s_ref[seq[hh], head[hh]] = s[hh] * jnp.exp(glast[hh]) + _dot(kd_t, u_pad)
        for hh in hs:
            r = lax.rsqrt(jnp.mean(o[hh] * o[hh], axis=-1, keepdims=True) + EPS)
            o_ref[sl[hh]] = (((o[hh] * r) * gn) * _silu(z_ref[sl[hh]].astype(F32))).astype(o_ref.dtype)
        return carry

    if n_chunks == 1:
        one_chunk(0, 0)
    else:
        lax.fori_loop(0, n_chunks, one_chunk, 0)


def _gdn(q, k, v, gates, z_arr, z_block, gn, s0, chunk, n_heads):
    b, l, w = q.shape
    tt = _tile(l, max(chunk, 256), chunk)
    nb = 2 if b % 2 == 0 else 1
    kern = functools.partial(_gdn_kernel, chunk=chunk, n_chunks=tt // chunk, n_heads=n_heads)
    row_spec = lambda n: pl.BlockSpec((nb, tt, n), lambda b_, t: (b_, t, 0))
    st_spec = pl.BlockSpec((nb, n_heads, LANES, LANES), lambda b_, t: (b_, 0, 0, 0))
    return pl.pallas_call(
        kern,
        grid=(b // nb, l // tt),
        in_specs=[row_spec(w), row_spec(w), row_spec(w), row_spec(LANES),
                  pl.BlockSpec((nb, tt, w), lambda b_, t: (b_, t, z_block)),
                  pl.BlockSpec((1, LANES), lambda b_, t: (0, 0)),
                  st_spec],
        out_specs=[row_spec(w), st_spec],
        out_shape=[jax.ShapeDtypeStruct((b, l, w), BF16),
                   jax.ShapeDtypeStruct(s0.shape, F32)],
        compiler_params=_cparams(("arbitrary", "arbitrary")),
        name="gdn",
    )(q, k, v, gates, z_arr, gn, s0)


def _merge_kernel(a_ref, g_ref, ga_ref, gg_ref, wpa_ref, wpg_ref, wo_ref, x_ref, gate_ref, gn_ref, o_ref):
    pa = _dot(a_ref[...], wpa_ref[...])
    pg = _dot(g_ref[...], wpg_ref[...])
    m = jax.nn.sigmoid(ga_ref[...].astype(F32)) * pa + jax.nn.sigmoid(gg_ref[...].astype(F32)) * pg
    y = _dot(m.astype(BF16), wo_ref[...])
    r = lax.rsqrt(jnp.mean(y * y, axis=-1, keepdims=True) + EPS)
    o_ref[...] = x_ref[...] + gate_ref[...] * ((y * r) * gn_ref[...])


def _merge(a_out, g_out, zg, x, mod, rows_per_seq, per_row, w_pa, w_pg, w_o, gn):
    r, d = x.shape
    w = a_out.shape[1]
    tm = _tile(rows_per_seq if not per_row else r, 256)
    resident = lambda shape: pl.BlockSpec(shape, lambda i: (0, 0), pipeline_mode=pl.Buffered(1))
    return pl.pallas_call(
        _merge_kernel,
        grid=(r // tm,),
        in_specs=[pl.BlockSpec((tm, w), lambda i: (i, 0)),
                  pl.BlockSpec((tm, w), lambda i: (i, 0)),
                  pl.BlockSpec((tm, d), lambda i: (i, 0)),
                  pl.BlockSpec((tm, d), lambda i: (i, 1)),
                  resident((w, d)), resident((w, d)), resident((d, d)),
                  pl.BlockSpec((tm, d), lambda i: (i, 0)),
                  _mod_spec(rows_per_seq, tm, d, 2, per_row),
                  pl.BlockSpec((1, d), lambda i: (0, 0))],
        out_specs=pl.BlockSpec((tm, d), lambda i: (i, 0)),
        out_shape=jax.ShapeDtypeStruct((r, d), F32),
        compiler_params=_cparams(("arbitrary",)),
        name="merge",
    )(a_out, g_out, zg, zg, w_pa, w_pg, w_o, x, mod, gn)


def _ffn_up_kernel(x_ref, g_ref, sh_ref, sc_ref, wg_ref, wu_ref, o_ref, h_ref):
    @pl.when(pl.program_id(1) == 0)
    def _():
        h_ref[...] = _modulated(x_ref[...], g_ref[...], sh_ref[...], sc_ref[...]).astype(BF16)

    h = h_ref[...]
    gt = _dot(h, wg_ref[...])
    up = _dot(h, wu_ref[...])
    o_ref[...] = (_silu(gt) * up).astype(o_ref.dtype)


def _ffn_up(x, mod, rows_per_seq, per_row, g, w_up):
    r, d = x.shape
    f = w_up.shape[1] // 2
    tn = _tile(f, 512, LANES)
    nj = f // tn
    tm = _tile(rows_per_seq if not per_row else r, 1024)
    return pl.pallas_call(
        _ffn_up_kernel,
        grid=(r // tm, nj),
        in_specs=[pl.BlockSpec((tm, d), lambda i, j: (i, 0)),
                  pl.BlockSpec((1, d), lambda i, j: (0, 0)),
                  _mod_spec(rows_per_seq, tm, d, 3, per_row),
                  _mod_spec(rows_per_seq, tm, d, 4, per_row),
                  pl.BlockSpec((d, tn), lambda i, j: (0, j)),
                  pl.BlockSpec((d, tn), lambda i, j: (0, nj + j))],
        out_specs=pl.BlockSpec((tm, tn), lambda i, j: (i, j)),
        out_shape=jax.ShapeDtypeStruct((r, f), BF16),
        scratch_shapes=[pltpu.VMEM((tm, d), BF16)],
        compiler_params=_cparams(("arbitrary", "arbitrary")),
        name="ffn_up",
    )(x, g, mod, mod, w_up, w_up)


def _ffn_down_kernel(a_ref, w_ref, x_ref, gate_ref, gn_ref, o_ref):
    y = _dot(a_ref[...], w_ref[...])
    r = lax.rsqrt(jnp.mean(y * y, axis=-1, keepdims=True) + EPS)
    o_ref[...] = x_ref[...] + gate_ref[...] * ((y * r) * gn_ref[...])


def _ffn_down(act, x, mod, rows_per_seq, per_row, w_down, gn):
    r, d = x.shape
    f = act.shape[1]
    tm = _tile(rows_per_seq if not per_row else r, 256)
    return pl.pallas_call(
        _ffn_down_kernel,
        grid=(r // tm,),
        in_specs=[pl.BlockSpec((tm, f), lambda i: (i, 0)),
                  pl.BlockSpec((f, d), lambda i: (0, 0), pipeline_mode=pl.Buffered(1)),
                  pl.BlockSpec((tm, d), lambda i: (i, 0)),
                  _mod_spec(rows_per_seq, tm, d, 5, per_row),
                  pl.BlockSpec((1, d), lambda i: (0, 0))],
        out_specs=pl.BlockSpec((tm, d), lambda i: (i, 0)),
        out_shape=jax.ShapeDtypeStruct((r, d), F32),
        compiler_params=_cparams(("arbitrary",)),
        name="ffn_down",
    )(act, w_down, x, mod, gn)


def _group(x, mod, per_row, attn_fn, conv_state, s0, chunk, valid_len, pw, n_heads):
    b, l, d = x.shape
    r = b * l
    w = n_heads * LANES
    xf = x.reshape(r, d)
    oattn, ok, ov, ogdn, ozg, osm = _inproj(xf, mod, l, per_row, pw["g_mix_pre"], pw["w_all"], pw["w_gates"],
                                            pw["b_main"], pw["w_small"], pw["b_small"], w)
    a_out = attn_fn(oattn)
    q, k, v, gates = _gdn_prep(ogdn.reshape(b, l, 3 * w), osm.reshape(b, l, LANES), conv_state,
                               pw["conv_w"], pw["a_log"], pw["dt_bias"], n_heads, valid_len)
    g_out, s_new = _gdn(q, k, v, gates, ozg.reshape(b, l, -1), (2 * d) // w, pw["gdn_norm"], s0, chunk, n_heads)
    x1 = _merge(a_out, g_out.reshape(r, w), ozg, xf, mod, l, per_row, pw["w_pa"], pw["w_pg"], pw["w_o"],
                pw["g_mix_post"])
    act = _ffn_up(x1, mod, l, per_row, pw["g_ffn_pre"], pw["w_up"])
    x2 = _ffn_down(act, x1, mod, l, per_row, pw["w_down"], pw["g_ffn_post"])
    kv = (ok.reshape(b, l, n_heads, LANES), ov.reshape(b, l, n_heads, LANES))
    return x2.reshape(b, l, d), kv, ogdn, s_new


def kernel(x_prompt, x_sample, c_prompt, c_sample, cache_k, cache_v, page_table, state_ssm, state_conv, w_ada, b_ada, norm_mix_pre, norm_mix_post, norm_ffn_pre, norm_ffn_post, w_in, b_in, conv_w, lambda_q1, lambda_k1, lambda_q2, lambda_k2, attn_subln, a_log, dt_bias, gdn_norm, w_proj_attn, w_proj_gdn, w_out, w_ffn_up, w_ffn_down):
    bsz, seq, d = x_prompt.shape
    db, s_new, _ = x_sample.shape
    depth = w_in.shape[0]
    n_heads = cache_k.shape[3]
    hd = cache_k.shape[4]
    assert hd == LANES and state_ssm.shape[-1] == LANES and state_ssm.shape[-2] == LANES
    assert state_ssm.shape[2] == n_heads
    w = n_heads * hd
    taps = conv_w.shape[1]
    chunk_p = 64 if seq % 64 == 0 else seq
    slopes = 2.0 ** (-8.0 * jnp.arange(1, n_heads + 1, dtype=F32) / n_heads)
    assert (2 * d) % w == 0
    s_pad = -(-s_new // 8) * 8

    xp, xs = x_prompt, x_sample
    outs = [[] for _ in range(8)]
    for l in range(depth):
        lam_init = 0.8 - 0.6 * math.exp(-0.3 * l)
        lam_vecs = [t[l].reshape(1, -1) for t in (lambda_q1, lambda_k1, lambda_q2, lambda_k2)]
        subg = attn_subln[l].reshape(1, hd)

        c_all = jnp.concatenate([c_prompt, c_sample], axis=0)
        c_all = jnp.pad(c_all, ((0, (-c_all.shape[0]) % 8), (0, 0)))
        mod = _adaln(c_all, w_ada[l], b_ada[l])
        mod_p = mod[:bsz].reshape(bsz, 1, 6 * d)
        mod_s = jnp.repeat(mod[bsz:bsz + db], s_pad, axis=0)

        wt, bi = w_in[l].T, b_in[l]
        n6 = 6 * w
        n7 = 7 * w
        w_all = wt.astype(BF16)
        pw = dict(
            w_all=w_all, w_gates=w_all[n7 + 2 * n_heads:],
            b_main=jnp.concatenate([bi[:n6], bi[n7 + 2 * n_heads:], bi[n6:n7]]).reshape(1, -1),
            w_small=jnp.pad(w_all[n7:n7 + 2 * n_heads], ((0, LANES - 2 * n_heads), (0, 0))),
            b_small=jnp.pad(bi[n7:n7 + 2 * n_heads], (0, LANES - 2 * n_heads)).reshape(1, LANES),
            g_mix_pre=norm_mix_pre[l].reshape(1, d), g_mix_post=norm_mix_post[l].reshape(1, d),
            g_ffn_pre=norm_ffn_pre[l].reshape(1, d), g_ffn_post=norm_ffn_post[l].reshape(1, d),
            conv_w=conv_w[l], a_log=a_log[l], dt_bias=dt_bias[l], gdn_norm=gdn_norm[l].reshape(1, hd),
            w_pa=w_proj_attn[l].astype(BF16), w_pg=w_proj_gdn[l].astype(BF16), w_o=w_out[l].astype(BF16),
            w_up=w_ffn_up[l].astype(BF16), w_down=w_ffn_down[l].astype(BF16),
        )

        def attn_p(oattn):
            o = _prompt_attention(oattn.reshape(bsz, seq, 3 * w), slopes, lam_vecs, subg, n_heads, lam_init)
            return o.reshape(bsz * seq, w)

        conv0 = jnp.zeros((bsz, taps - 1, 3 * w), F32)
        s0 = jnp.zeros((bsz, n_heads, hd, hd), F32)
        xp, kv_p, raw_p, ssm_p = _group(xp, mod_p, False, attn_p, conv0, s0, chunk_p, seq, pw, n_heads)

        xs_pad = jnp.pad(xs, ((0, 0), (0, s_pad - s_new), (0, 0)))

        def attn_s(oattn):
            qkv = oattn.reshape(db, s_pad, 3, n_heads, hd)[:, :s_new]
            qh = qkv[:, :, 0].transpose(0, 2, 1, 3)
            qrows = jax.vmap(lambda t: _split_maps(t, (hd // 2) ** -0.5))(qh.reshape(db, n_heads * s_new, hd))
            knew = qkv[:, :, 1].reshape(db, s_new * n_heads, hd)
            vnew = qkv[:, :, 2].reshape(db, s_new * n_heads, hd)
            o = _sample_attention(qrows, knew, vnew, cache_k, cache_v, l, page_table, slopes, lam_vecs, subg,
                                  lam_init)
            o = o.reshape(db, n_heads, s_new, hd).transpose(0, 2, 1, 3).reshape(db, s_new, w)
            return jnp.pad(o, ((0, 0), (0, s_pad - s_new), (0, 0))).reshape(db * s_pad, w).astype(BF16)

        xs_full, kv_s, raw_s, ssm_s = _group(xs_pad, mod_s, True, attn_s, state_conv[l], state_ssm[l], s_pad,
                                             s_new, pw, n_heads)
        xs = xs_full[:, :s_new]

        raw_p = raw_p.reshape(bsz, seq, 3 * w)
        raw_s = jnp.concatenate([state_conv[l], raw_s.reshape(db, s_pad, 3 * w)[:, :s_new]], axis=1)
        for lst, val in zip(outs, (kv_p[0], kv_p[1], kv_s[0][:, :s_new], kv_s[1][:, :s_new], ssm_p, ssm_s,
                                   raw_p[:, seq - (taps - 1):], raw_s[:, s_new:])):
            lst.append(val)

    return (xp, xs) + tuple(jnp.stack(o) for o in outs)
```

```python
import functools
import math

import jax
import jax.numpy as jnp
from jax import lax
from jax.experimental import pallas as pl
from jax.experimental.pallas import tpu as pltpu

F32 = jnp.float32
BF16 = jnp.bfloat16
EPS = 1e-6
NEG = -1e30
LANES = 128
VMEM_LIMIT = 56 * 1024 * 1024
HI = lax.Precision.HIGHEST
LOG2E = math.log2(math.e)


def _cparams(sem):
    return pltpu.CompilerParams(dimension_semantics=sem, vmem_limit_bytes=VMEM_LIMIT)


def _tile(n, target, mult=8):
    if n <= target:
        return n
    for t in range(target, 0, -1):
        if n % t == 0 and t % mult == 0:
            return t
    return n


def _silu(x):
    return x * jax.nn.sigmoid(x)


def _dot(a, b):
    return jnp.dot(a, b, preferred_element_type=F32)


def _split(a):
    hi = a.astype(BF16)
    return hi, (a - hi.astype(F32)).astype(BF16)


def _dot3(a, b):
    ah, al = _split(a)
    bh, bl = _split(b)
    return _dot(ah, bh) + (_dot(ah, bl) + _dot(al, bh))


def _dot_nt(a, b, precision=None):
    return lax.dot_general(a, b, (((1,), (1,)), ((), ())), preferred_element_type=F32,
                           precision=precision)


def _dot_tn(a, b):
    return lax.dot_general(a, b, (((0,), (0,)), ((), ())), preferred_element_type=F32)


def _adaln_kernel(c_ref, w_ref, b_ref, o_ref):
    a = _silu(c_ref[...]).astype(BF16)
    o_ref[...] = _dot(a, w_ref[...].astype(BF16)) + b_ref[...]


def _adaln(c, w, b):
    m, d = c.shape
    n = w.shape[1]
    tn = _tile(n, 1024, LANES)
    return pl.pallas_call(
        _adaln_kernel,
        grid=(n // tn,),
        in_specs=[pl.BlockSpec((m, d), lambda j: (0, 0)),
                  pl.BlockSpec((d, tn), lambda j: (0, j)),
                  pl.BlockSpec((1, tn), lambda j: (0, j))],
        out_specs=pl.BlockSpec((m, tn), lambda j: (0, j)),
        out_shape=jax.ShapeDtypeStruct((m, n), F32),
        compiler_params=_cparams(("arbitrary",)),
        name="adaln",
    )(c, w, b.reshape(1, n))


def _modulated(x, g, shift, scale):
    r = lax.rsqrt(jnp.mean(x * x, axis=-1, keepdims=True) + EPS)
    return ((x * r) * g) * (1.0 + scale) + shift


def _inproj_kernel(x_ref, g_ref, sh_ref, sc_ref, wa_ref, wg_ref, b_ref, ws_ref, bs_ref,
                   oattn_ref, ok_ref, ov_ref, ogdn_ref, ozg_ref, osm_ref, h_ref, *, spw):
    j = pl.program_id(1)
    nj = pl.num_programs(1)
    hps = ok_ref.shape[1] // spw

    @pl.when(j == 0)
    def _():
        hb = _modulated(x_ref[...], g_ref[...], sh_ref[...], sc_ref[...]).astype(BF16)
        h_ref[...] = hb
        osm_ref[...] = _dot_nt(hb, ws_ref[...]) + bs_ref[...]

    def project(w_ref):
        return _dot_nt(h_ref[...], w_ref[...]) + b_ref[...]

    def heads_to_sublanes(y, o_ref, first):
        o_ref[:, first:first + hps, :] = pltpu.einshape("t(hd)->thd", y, h=hps)

    @pl.when(j < 3 * spw)
    def _():
        y = project(wa_ref)
        oattn_ref[...] = y.astype(BF16)
        for part in range(spw):
            @pl.when(j == spw + part)
            def _():
                heads_to_sublanes(y, ok_ref, part * hps)

            @pl.when(j == 2 * spw + part)
            def _():
                heads_to_sublanes(y, ov_ref, part * hps)

    @pl.when((j >= 3 * spw) & (j < 6 * spw))
    def _():
        ogdn_ref[...] = project(wa_ref)

    @pl.when((j >= 6 * spw) & (j < nj - spw))
    def _():
        ozg_ref[...] = project(wg_ref).astype(BF16)

    @pl.when(j >= nj - spw)
    def _():
        ozg_ref[...] = project(wa_ref).astype(BF16)


def _mod_spec(rows_per_seq, tm, d, chunk, per_row):
    if per_row:
        return pl.BlockSpec((tm, d), lambda i, *_: (i, chunk))
    return pl.BlockSpec((None, 1, d), lambda i, *_: ((i * tm) // rows_per_seq, 0, chunk))


def _inproj(x, mod, rows_per_seq, per_row, g, w_all, w_gates, b_main, w_small, b_small, width):
    r, d = x.shape
    n_heads = width // LANES
    tm = _tile(rows_per_seq if not per_row else r, 1024)
    spw = 2 if (n_heads % 2 == 0 and tm > 512) else 1
    tn = width // spw
    n_gate = w_gates.shape[0] // tn
    nj = 7 * spw + n_gate
    grid = (r // tm, nj)
    clip = lambda j, lo, hi: jnp.minimum(jnp.maximum(j - lo, 0), hi - lo)
    return pl.pallas_call(
        functools.partial(_inproj_kernel, spw=spw),
        grid=grid,
        in_specs=[pl.BlockSpec((tm, d), lambda i, j: (i, 0), pipeline_mode=pl.Buffered(1)),
                  pl.BlockSpec((1, d), lambda i, j: (0, 0)),
                  _mod_spec(rows_per_seq, tm, d, 0, per_row),
                  _mod_spec(rows_per_seq, tm, d, 1, per_row),
                  pl.BlockSpec((tn, d), lambda i, j: (jnp.where(j < 6 * spw, j, clip(j, nj - spw, nj - 1) + 6 * spw), 0)),
                  pl.BlockSpec((tn, d), lambda i, j: (clip(j, 6 * spw, 6 * spw + n_gate - 1), 0)),
                  pl.BlockSpec((1, tn), lambda i, j: (0, j)),
                  pl.BlockSpec((LANES, d), lambda i, j: (0, 0)),
                  pl.BlockSpec((1, LANES), lambda i, j: (0, 0))],
        out_specs=[pl.BlockSpec((tm, tn), lambda i, j: (i, clip(j, 0, 3 * spw - 1))),
                   pl.BlockSpec((tm, n_heads, LANES), lambda i, j: (i, 0, 0)),
                   pl.BlockSpec((tm, n_heads, LANES), lambda i, j: (i, 0, 0)),
                   pl.BlockSpec((tm, tn), lambda i, j: (i, clip(j, 3 * spw, 6 * spw - 1))),
                   pl.BlockSpec((tm, tn), lambda i, j: (i, clip(j, 6 * spw, nj - 1))),
                   pl.BlockSpec((tm, LANES), lambda i, j: (i, 0))],
        out_shape=[jax.ShapeDtypeStruct((r, 3 * width), BF16),
                   jax.ShapeDtypeStruct((r, n_heads, LANES), F32),
                   jax.ShapeDtypeStruct((r, n_heads, LANES), F32),
                   jax.ShapeDtypeStruct((r, 3 * width), F32),
                   jax.ShapeDtypeStruct((r, (nj - 6 * spw) * tn), BF16),
                   jax.ShapeDtypeStruct((r, LANES), F32)],
        scratch_shapes=[pltpu.VMEM((tm, d), BF16)],
        compiler_params=_cparams(("arbitrary", "arbitrary")),
        name="inproj",
    )(x, g, mod, mod, w_all, w_gates, b_main, w_small, b_small)


def _lambda_value(lq1, lk1, lq2, lk2, lam_init):
    s1 = jnp.sum(lq1 * lk1, axis=-1, keepdims=True)
    s2 = jnp.sum(lq2 * lk2, axis=-1, keepdims=True)
    return jnp.exp(s1) - jnp.exp(s2) + lam_init


def _diff_finish(acc, l, lam, subg, lam_init, half):
    o1 = acc[:half] / l[:half]
    o2 = acc[half:] / l[half:]
    a = o1 - lam * o2
    r = lax.rsqrt(jnp.mean(a * a, axis=-1, keepdims=True) + EPS)
    return ((a * r) * subg) * (1.0 - lam_init)


def _split_maps(q, scale):
    lane = lax.broadcasted_iota(jnp.int32, q.shape, 1)
    qs = (q.astype(F32) * scale).astype(q.dtype)
    zero = jnp.zeros_like(qs)
    half = q.shape[-1] // 2
    return jnp.concatenate([jnp.where(lane < half, qs, zero), jnp.where(lane >= half, qs, zero)], axis=0)


ONES_ROWS = 16


def _pattn_kernel(slopes_ref, lq1_ref, lk1_ref, lq2_ref, lk2_ref, subg_ref, q_ref, k_ref, v_ref,
                  o_ref, vt_ref, bias_ref, m_ref, acc_ref, sa_ref, sb_ref, *, tq, hpb, lam_init, scale):
    hg = pl.program_id(1)
    qi = pl.program_id(2)
    seq = k_ref.shape[0]
    heads = range(hpb)
    lanes = [slice(e * LANES, (e + 1) * LANES) for e in heads]
    slope = [slopes_ref[hg * hpb + e] * LOG2E for e in heads]

    @pl.when(qi == 0)
    def _():
        for e in heads:
            for c in range(seq // tq):
                blk = v_ref[c * tq:(c + 1) * tq, lanes[e]].astype(F32)
                vt_ref[e, 0:LANES, c * tq:(c + 1) * tq] = blk.T.astype(BF16)
            vt_ref[e, LANES:, :] = jnp.ones((ONES_ROWS, seq), BF16)
            bias_ref[e] = slope[e] * lax.broadcasted_iota(jnp.int32, bias_ref.shape[1:], 0).astype(F32)

    qq = [_split_maps(q_ref[:, lanes[e]], scale) for e in heads]
    m_ref[...] = jnp.full(m_ref.shape, NEG, F32)
    acc_ref[...] = jnp.zeros(acc_ref.shape, F32)

    def scores(c, dst_ref):
        start = pl.multiple_of(c * tq, tq)
        for e in heads:
            dst_ref[e] = _dot_nt(k_ref[pl.ds(start, tq), lanes[e]], qq[e]) + bias_ref[e]

    def chunk(c, src_ref, masked):
        start = pl.multiple_of(c * tq, tq)
        for e in heads:
            st = src_ref[e]
            if masked:
                kr = lax.broadcasted_iota(jnp.int32, st.shape, 0)
                qc = lax.broadcasted_iota(jnp.int32, st.shape, 1)
                qc = jnp.where(qc >= tq, qc - tq, qc)
                st = jnp.where(qc >= kr, st, NEG)
            m_prev = m_ref[e] - slope[e] * tq
            m_new = jnp.maximum(m_prev, jnp.max(st, axis=0, keepdims=True))
            alpha = jnp.exp2(m_prev - m_new)
            p = jnp.exp2(st - m_new).astype(BF16)
            acc_ref[e] = _dot(vt_ref[e, :, pl.ds(start, tq)], p) + alpha * acc_ref[e]
            m_ref[e] = m_new

    scores(0, sa_ref)

    def pair(i, carry):
        c = 2 * i
        scores(c + 1, sb_ref)
        chunk(c, sa_ref, False)
        scores(c + 2, sa_ref)
        chunk(c + 1, sb_ref, False)
        return carry

    lax.fori_loop(0, qi // 2, pair, 0)

    @pl.when(qi % 2 == 1)
    def _():
        scores(qi, sb_ref)
        chunk(qi - 1, sa_ref, False)
        chunk(qi, sb_ref, True)

    @pl.when(qi % 2 == 0)
    def _():
        chunk(qi, sa_ref, True)

    lam = _lambda_value(lq1_ref[...], lk1_ref[...], lq2_ref[...], lk2_ref[...], lam_init)
    for e in heads:
        acc = acc_ref[e]
        on = acc[:LANES] / acc[LANES:LANES + 1]
        a = (on[:, :tq] - lam * on[:, tq:]).T
        r = lax.rsqrt(jnp.mean(a * a, axis=-1, keepdims=True) + EPS)
        o_ref[:, lanes[e]] = (((a * r) * subg_ref[...]) * (1.0 - lam_init)).astype(o_ref.dtype)


def _prompt_attention(qkv, slopes, lam_vecs, subg, n_heads, lam_init):
    b, l, _ = qkv.shape
    tq = _tile(l, 256)
    hd = LANES
    hpb = 2 if n_heads % 2 == 0 else 1
    ng = n_heads // hpb
    vec = lambda n: pl.BlockSpec((1, n), lambda b_, h, i: (0, 0))
    kern = functools.partial(_pattn_kernel, tq=tq, hpb=hpb, lam_init=lam_init, scale=(hd // 2) ** -0.5 * LOG2E)
    return pl.pallas_call(
        kern,
        grid=(b, ng, l // tq),
        in_specs=[pl.BlockSpec(memory_space=pltpu.SMEM),
                  vec(hd // 2), vec(hd // 2), vec(hd // 2), vec(hd // 2), vec(hd),
                  pl.BlockSpec((None, tq, hpb * hd), lambda b_, h, i: (b_, i, h)),
                  pl.BlockSpec((None, l, hpb * hd), lambda b_, h, i: (b_, 0, ng + h)),
                  pl.BlockSpec((None, l, hpb * hd), lambda b_, h, i: (b_, 0, 2 * ng + h))],
        out_specs=pl.BlockSpec((None, tq, hpb * hd), lambda b_, h, i: (b_, i, h)),
        out_shape=jax.ShapeDtypeStruct((b, l, n_heads * hd), BF16),
        scratch_shapes=[pltpu.VMEM((hpb, hd + ONES_ROWS, l), BF16), pltpu.VMEM((hpb, tq, 2 * tq), F32),
                        pltpu.VMEM((hpb, 1, 2 * tq), F32), pltpu.VMEM((hpb, hd + ONES_ROWS, 2 * tq), F32),
                        pltpu.VMEM((hpb, tq, 2 * tq), F32), pltpu.VMEM((hpb, tq, 2 * tq), F32)],
        compiler_params=_cparams(("arbitrary", "arbitrary", "arbitrary")),
        name="prompt_attn",
    )(slopes, *lam_vecs, subg, qkv, qkv, qkv)


PAGE_RING = 3


def _sattn_kernel(pt_ref, slope_ref, bias_ref, biasn_ref, lq1_ref, lk1_ref, lq2_ref, lk2_ref, subg_ref,
                  q_ref, kn_ref, vn_ref, ck_hbm, cv_hbm, o_ref, kbuf, vbuf, sem, m_ref, l_ref, acc_ref,
                  *, layer, n_pages_step, n_seq, ng, page, lam_init):
    b = pl.program_id(0)
    g = pl.program_id(1)
    n = b * ng + g
    total = n_seq * ng

    def group_copies(seq, grp, slot):
        copies = []
        for i in range(n_pages_step):
            pid = pt_ref[seq, grp * n_pages_step + i]
            copies.append(pltpu.make_async_copy(ck_hbm.at[layer, pid], kbuf.at[slot, i], sem.at[0, slot]))
            copies.append(pltpu.make_async_copy(cv_hbm.at[layer, pid], vbuf.at[slot, i], sem.at[1, slot]))
        return copies

    @pl.when(n == 0)
    def _():
        for d in range(min(PAGE_RING - 1, total)):
            for cp in group_copies(d // ng, d % ng, d):
                cp.start()

    nxt = n + (PAGE_RING - 1)

    @pl.when(nxt < total)
    def _():
        for cp in group_copies(lax.div(nxt, ng), lax.rem(nxt, ng), lax.rem(nxt, PAGE_RING)):
            cp.start()

    slot = lax.rem(n, PAGE_RING)
    for cp in group_copies(b, g, slot):
        cp.wait()
    k_refs = [kbuf.at[slot, i] for i in range(n_pages_step)]
    v_refs = [vbuf.at[slot, i] for i in range(n_pages_step)]

    @pl.when(g == 0)
    def _():
        m_ref[...] = jnp.full(m_ref.shape, NEG, F32)
        l_ref[...] = jnp.zeros(l_ref.shape, F32)
        acc_ref[...] = jnp.zeros(acc_ref.shape, F32)

    qq = q_ref[...]
    step = slope_ref[...] * (page * n_pages_step)

    def update(s, vs):
        m_prev = m_ref[...] - step
        m_new = jnp.maximum(m_prev, jnp.max(s, axis=-1, keepdims=True))
        alpha = jnp.exp(m_prev - m_new)
        p = jnp.exp(s - m_new)
        l_ref[...] = alpha * l_ref[...] + jnp.sum(p, axis=-1, keepdims=True)
        pb = p.astype(BF16)
        pv = None
        off = 0
        for vf in vs:
            part = _dot(pb[:, off:off + vf.shape[0]], vf)
            pv = part if pv is None else pv + part
            off += vf.shape[0]
        acc_ref[...] = alpha * acc_ref[...] + pv
        m_ref[...] = m_new

    s = jnp.concatenate([_dot_nt(qq, k_refs[i][...].reshape(-1, LANES).astype(BF16))
                         for i in range(n_pages_step)], axis=1) + bias_ref[...]
    update(s, [v_refs[i][...].reshape(-1, LANES).astype(BF16) for i in range(n_pages_step)])

    @pl.when(g == ng - 1)
    def _():
        update(_dot_nt(qq, kn_ref[...]) + biasn_ref[...], [vn_ref[...]])
        lam = _lambda_value(lq1_ref[...], lk1_ref[...], lq2_ref[...], lk2_ref[...], lam_init)
        half = acc_ref.shape[0] // 2
        o_ref[...] = _diff_finish(acc_ref[...], l_ref[...], lam, subg_ref[...], lam_init, half)


def _sample_attention(qrows, knew, vnew, cache_k, cache_v, layer, page_table, slopes, lam_vecs, subg, lam_init):
    db, nrow, hd = qrows.shape
    _, n_phys, page, n_heads, _ = cache_k.shape
    s_new = knew.shape[1] // n_heads
    n_pages = page_table.shape[1]
    pps = 8 if n_pages % 8 == 0 else 1
    r = jnp.arange(nrow)
    r_head = (r // s_new) % n_heads
    r_tok = r % s_new
    slope_rows = slopes[r_head][:, None]
    c = jnp.arange(pps * page * n_heads)
    bias = jnp.where(r_head[:, None] == (c % n_heads)[None, :], slope_rows * (c // n_heads)[None, :].astype(F32), NEG)
    cn = jnp.arange(s_new * n_heads)
    ok = (r_head[:, None] == (cn % n_heads)[None, :]) & ((cn // n_heads)[None, :] <= r_tok[:, None])
    biasn = jnp.where(ok, slope_rows * (cn // n_heads)[None, :].astype(F32), NEG)

    const = lambda shape: pl.BlockSpec(shape, lambda b, g, pt: (0,) * len(shape))
    kern = functools.partial(_sattn_kernel, layer=layer, n_pages_step=pps, n_seq=db, ng=n_pages // pps,
                             page=float(page), lam_init=lam_init)
    ring = (PAGE_RING, pps, page, n_heads, hd)
    grid_spec = pltpu.PrefetchScalarGridSpec(
        num_scalar_prefetch=1,
        grid=(db, n_pages // pps),
        in_specs=[const((nrow, 1)), const((nrow, pps * page * n_heads)), const((nrow, s_new * n_heads)),
                  const((1, hd // 2)), const((1, hd // 2)), const((1, hd // 2)), const((1, hd // 2)), const((1, hd)),
                  pl.BlockSpec((None, nrow, hd), lambda b, g, pt: (b, 0, 0)),
                  pl.BlockSpec((None, s_new * n_heads, hd), lambda b, g, pt: (b, 0, 0)),
                  pl.BlockSpec((None, s_new * n_heads, hd), lambda b, g, pt: (b, 0, 0)),
                  pl.BlockSpec(memory_space=pl.ANY), pl.BlockSpec(memory_space=pl.ANY)],
        out_specs=pl.BlockSpec((None, nrow // 2, hd), lambda b, g, pt: (b, 0, 0)),
        scratch_shapes=[pltpu.VMEM(ring, F32), pltpu.VMEM(ring, F32), pltpu.SemaphoreType.DMA((2, PAGE_RING)),
                        pltpu.VMEM((nrow, 1), F32), pltpu.VMEM((nrow, 1), F32), pltpu.VMEM((nrow, hd), F32)],
    )
    return pl.pallas_call(
        kern,
        grid_spec=grid_spec,
        out_shape=jax.ShapeDtypeStruct((db, nrow // 2, hd), F32),
        compiler_params=_cparams(("arbitrary", "arbitrary")),
        name="sample_attn",
    )(page_table, slope_rows, bias, biasn, *lam_vecs, subg, qrows, knew, vnew, cache_k, cache_v)


def _gdn_prep_kernel(x_ref, sm_ref, st_ref, cw_ref, alog_ref, dtb_ref, q_ref, k_ref, v_ref, gate_ref, pad_ref,
                     *, tt, n_heads, valid_len, taps):
    t = pl.program_id(1)
    halo = 8
    w3 = n_heads * LANES

    @pl.when(t == 0)
    def _():
        pad_ref[0:halo, :] = jnp.zeros((halo, pad_ref.shape[1]), F32)
        pad_ref[halo - (taps - 1):halo, :] = st_ref[...]

    pad_ref[halo:halo + tt, :] = x_ref[...]
    y = pad_ref[halo:halo + tt, :] * cw_ref[taps - 1:taps, :]
    for j in range(taps - 1):
        off = halo - (taps - 1) + j
        y = y + pad_ref[off:off + tt, :] * cw_ref[j:j + 1, :]
    y = _silu(y)
    pad_ref[0:halo, :] = pad_ref[tt:tt + halo, :]

    def l2n(z):
        return z * lax.rsqrt(jnp.sum(z * z, axis=-1, keepdims=True) + EPS)

    for hh in range(n_heads):
        sl = slice(hh * LANES, (hh + 1) * LANES)
        q_ref[:, sl] = l2n(y[:, sl]) * (LANES ** -0.5)
        k_ref[:, sl] = l2n(y[:, w3 + hh * LANES:w3 + (hh + 1) * LANES])
    v_ref[...] = y[:, 2 * w3:]

    x = sm_ref[...]
    lane = lax.broadcasted_iota(jnp.int32, x.shape, 1)
    row = lax.broadcasted_iota(jnp.int32, x.shape, 0) + t * tt
    beta = jax.nn.sigmoid(x)
    z = x + dtb_ref[...]
    softplus = jnp.maximum(z, 0.0) + jnp.log(1.0 + jnp.exp(-jnp.abs(z)))
    gdec = -jnp.exp(alog_ref[...]) * softplus
    out = jnp.where(lane < n_heads, beta, gdec)
    gate_ref[...] = jnp.where((row < valid_len) & (lane < 2 * n_heads), out, 0.0)


def _gdn_prep(x, small, state, conv_w, a_log, dt_bias, n_heads, valid_len):
    b, l, ch = x.shape
    taps = conv_w.shape[0]
    tt = _tile(l, 256)
    w = n_heads * LANES
    lanes = jnp.arange(LANES)
    in_g = (lanes >= n_heads) & (lanes < 2 * n_heads)
    alog_v = jnp.where(in_g, a_log[jnp.clip(lanes - n_heads, 0, n_heads - 1)], 0.0).reshape(1, LANES)
    dtb_v = jnp.where(in_g, dt_bias[jnp.clip(lanes - n_heads, 0, n_heads - 1)], 0.0).reshape(1, LANES)
    kern = functools.partial(_gdn_prep_kernel, tt=tt, n_heads=n_heads, valid_len=valid_len, taps=taps)
    row_spec = lambda n: pl.BlockSpec((None, tt, n), lambda b_, t: (b_, t, 0))
    return pl.pallas_call(
        kern,
        grid=(b, l // tt),
        in_specs=[row_spec(ch), row_spec(LANES),
                  pl.BlockSpec((None, taps - 1, ch), lambda b_, t: (b_, 0, 0)),
                  pl.BlockSpec((taps, ch), lambda b_, t: (0, 0)),
                  pl.BlockSpec((1, LANES), lambda b_, t: (0, 0)),
                  pl.BlockSpec((1, LANES), lambda b_, t: (0, 0))],
        out_specs=[row_spec(w), row_spec(w), row_spec(w), row_spec(LANES)],
        out_shape=[jax.ShapeDtypeStruct((b, l, w), F32)] * 3 + [jax.ShapeDtypeStruct((b, l, LANES), F32)],
        scratch_shapes=[pltpu.VMEM((tt + 8, ch), F32)],
        compiler_params=_cparams(("arbitrary", "arbitrary")),
        name="gdn_prep",
    )(x, small, state, conv_w, alog_v, dtb_v)


def _gdn_kernel(q_ref, k_ref, v_ref, gate_ref, z_ref, gn_ref, s0_ref, o_ref, s_ref, *, chunk, n_chunks, n_heads):
    t = pl.program_id(1)
    nb = q_ref.shape[0]

    @pl.when(t == 0)
    def _():
        s_ref[...] = s0_ref[...]

    c_ = chunk
    ri = lax.broadcasted_iota(jnp.int32, (c_, c_), 0)
    ci = lax.broadcasted_iota(jnp.int32, (c_, c_), 1)
    causal = ri >= ci
    strict = ri > ci
    tri = jnp.where(causal, 1.0, 0.0).astype(F32)
    eye = jnp.where(ri == ci, 1.0, 0.0).astype(F32)
    n_sq = max(int(math.log2(c_)) - 1, 0)
    gn = gn_ref[...]
    sel_r = lax.broadcasted_iota(jnp.int32, (8, LANES), 0)
    sel_c = lax.broadcasted_iota(jnp.int32, (8, LANES), 1)
    sel = jnp.where(sel_c == sel_r + n_heads, 1.0, 0.0).astype(F32)
    zpad = jnp.zeros((LANES - c_, LANES), F32)

    def one_chunk(c, carry):
        rows = pl.ds(c * c_ if n_chunks == 1 else pl.multiple_of(c * c_, c_), c_)
        gates = [gate_ref[bb, rows, :] for bb in range(nb)]
        gcum = [jnp.dot(tri, gates[bb], preferred_element_type=F32, precision=HI) for bb in range(nb)]
        gcum_t = [_dot_nt(sel, gcum[bb], precision=HI) for bb in range(nb)]
        hs = range(nb * n_heads)
        seq = [u_ // n_heads for u_ in hs]
        head = [u_ % n_heads for u_ in hs]
        sl = [(seq[u_], rows, slice(head[u_] * LANES, (head[u_] + 1) * LANES)) for u_ in hs]
        q = [q_ref[sl[hh]] for hh in hs]
        k = [k_ref[sl[hh]] for hh in hs]
        beta = [gates[seq[hh]][:, head[hh]:head[hh] + 1] for hh in hs]
        gc = [gcum[seq[hh]][:, n_heads + head[hh]:n_heads + head[hh] + 1] for hh in hs]
        glast = [gcum[seq[hh]][c_ - 1:c_, n_heads + head[hh]:n_heads + head[hh] + 1] for hh in hs]
        decay = [jnp.exp(jnp.where(causal, gc[hh] - gcum_t[seq[hh]][head[hh]:head[hh] + 1, :], NEG)) for hh in hs]
        kb = [k[hh].astype(BF16) for hh in hs]
        qk_kk = [_dot_nt(jnp.concatenate([kb[hh], q[hh].astype(BF16)], axis=0), kb[hh]) for hh in hs]
        a = [jnp.where(strict, beta[hh] * qk_kk[hh][:c_] * decay[hh], 0.0) for hh in hs]
        tinv = [eye - a[hh] for hh in hs]
        ap = a
        for _ in range(n_sq):
            ap = [_dot3(ap[hh], ap[hh]) for hh in hs]
            tinv = [tinv[hh] + _dot3(tinv[hh], ap[hh]) for hh in hs]
        eg = [jnp.exp(gc[hh]) for hh in hs]
        rhs = [jnp.concatenate([v_ref[sl[hh]] * beta[hh], k[hh] * (beta[hh] * eg[hh])], axis=1).astype(BF16)
               for hh in hs]
        uw = [_dot(tinv[hh].astype(BF16), rhs[hh]) for hh in hs]
        s = [s_ref[seq[hh], head[hh]] for hh in hs]
        wq = [_dot(jnp.concatenate([uw[hh][:, LANES:], q[hh] * eg[hh]], axis=0).astype(BF16), s[hh].astype(BF16))
              for hh in hs]
        u = [uw[hh][:, :LANES] - wq[hh][:c_] for hh in hs]
        ub = [u[hh].astype(BF16) for hh in hs]
        o = [wq[hh][c_:] + _dot((qk_kk[hh][c_:] * decay[hh]).astype(BF16), ub[hh]) for hh in hs]
        for hh in hs:
            kd = k[hh] * jnp.exp(glast[hh] - gc[hh])
            kd_t = jnp.concatenate([kd, zpad], axis=0).T.astype(BF16)
            u_pad = jnp.concatenate([u[hh], zpad], axis=0).astype(BF16)
            s_ref[seq[hh], head[hh]] = s[hh] * jnp.exp(glast[hh]) + _dot(kd_t, u_pad)
        for hh in hs:
            r = lax.rsqrt(jnp.mean(o[hh] * o[hh], axis=-1, keepdims=True) + EPS)
            o_ref[sl[hh]] = (((o[hh] * r) * gn) * _silu(z_ref[sl[hh]].astype(F32))).astype(o_ref.dtype)
        return carry

    if n_chunks == 1:
        one_chunk(0, 0)
    else:
        lax.fori_loop(0, n_chunks, one_chunk, 0)


def _gdn(q, k, v, gates, z_arr, z_block, gn, s0, chunk, n_heads):
    b, l, w = q.shape
    tt = _tile(l, max(chunk, 256), chunk)
    nb = 2 if b % 2 == 0 else 1
    kern = functools.partial(_gdn_kernel, chunk=chunk, n_chunks=tt // chunk, n_heads=n_heads)
    row_spec = lambda n: pl.BlockSpec((nb, tt, n), lambda b_, t: (b_, t, 0))
    st_spec = pl.BlockSpec((nb, n_heads, LANES, LANES), lambda b_, t: (b_, 0, 0, 0))
    return pl.pallas_call(
        kern,
        grid=(b // nb, l // tt),
        in_specs=[row_spec(w), row_spec(w), row_spec(w), row_spec(LANES),
                  pl.BlockSpec((nb, tt, w), lambda b_, t: (b_, t, z_block)),
                  pl.BlockSpec((1, LANES), lambda b_, t: (0, 0)),
                  st_spec],
        out_specs=[row_spec(w), st_spec],
        out_shape=[jax.ShapeDtypeStruct((b, l, w), BF16),
                   jax.ShapeDtypeStruct(s0.shape, F32)],
        compiler_params=_cparams(("arbitrary", "arbitrary")),
        name="gdn",
    )(q, k, v, gates, z_arr, gn, s0)


def _merge_kernel(a_ref, g_ref, ga_ref, gg_ref, wpa_ref, wpg_ref, wo_ref, x_ref, gate_ref, gn_ref, o_ref):
    pa = _dot(a_ref[...], wpa_ref[...])
    pg = _dot(g_ref[...], wpg_ref[...])
    m = jax.nn.sigmoid(ga_ref[...].astype(F32)) * pa + jax.nn.sigmoid(gg_ref[...].astype(F32)) * pg
    y = _dot(m.astype(BF16), wo_ref[...])
    r = lax.rsqrt(jnp.mean(y * y, axis=-1, keepdims=True) + EPS)
    o_ref[...] = x_ref[...] + gate_ref[...] * ((y * r) * gn_ref[...])


def _merge(a_out, g_out, zg, x, mod, rows_per_seq, per_row, w_pa, w_pg, w_o, gn):
    r, d = x.shape
    w = a_out.shape[1]
    tm = _tile(rows_per_seq if not per_row else r, 256)
    resident = lambda shape: pl.BlockSpec(shape, lambda i: (0, 0), pipeline_mode=pl.Buffered(1))
    return pl.pallas_call(
        _merge_kernel,
        grid=(r // tm,),
        in_specs=[pl.BlockSpec((tm, w), lambda i: (i, 0)),
                  pl.BlockSpec((tm, w), lambda i: (i, 0)),
                  pl.BlockSpec((tm, d), lambda i: (i, 0)),
                  pl.BlockSpec((tm, d), lambda i: (i, 1)),
                  resident((w, d)), resident((w, d)), resident((d, d)),
                  pl.BlockSpec((tm, d), lambda i: (i, 0)),
                  _mod_spec(rows_per_seq, tm, d, 2, per_row),
                  pl.BlockSpec((1, d), lambda i: (0, 0))],
        out_specs=pl.BlockSpec((tm, d), lambda i: (i, 0)),
        out_shape=jax.ShapeDtypeStruct((r, d), F32),
        compiler_params=_cparams(("arbitrary",)),
        name="merge",
    )(a_out, g_out, zg, zg, w_pa, w_pg, w_o, x, mod, gn)


def _ffn_up_kernel(x_ref, g_ref, sh_ref, sc_ref, wg_ref, wu_ref, o_ref, h_ref):
    @pl.when(pl.program_id(1) == 0)
    def _():
        h_ref[...] = _modulated(x_ref[...], g_ref[...], sh_ref[...], sc_ref[...]).astype(BF16)

    h = h_ref[...]
    gt = _dot(h, wg_ref[...].astype(BF16))
    up = _dot(h, wu_ref[...].astype(BF16))
    o_ref[...] = (_silu(gt) * up).astype(o_ref.dtype)


def _ffn_up(x, mod, rows_per_seq, per_row, g, w_up):
    r, d = x.shape
    f = w_up.shape[1] // 2
    tn = _tile(f, 512, LANES)
    nj = f // tn
    tm = _tile(rows_per_seq if not per_row else r, 1024)
    return pl.pallas_call(
        _ffn_up_kernel,
        grid=(r // tm, nj),
        in_specs=[pl.BlockSpec((tm, d), lambda i, j: (i, 0)),
                  pl.BlockSpec((1, d), lambda i, j: (0, 0)),
                  _mod_spec(rows_per_seq, tm, d, 3, per_row),
                  _mod_spec(rows_per_seq, tm, d, 4, per_row),
                  pl.BlockSpec((d, tn), lambda i, j: (0, j)),
                  pl.BlockSpec((d, tn), lambda i, j: (0, nj + j))],
        out_specs=pl.BlockSpec((tm, tn), lambda i, j: (i, j)),
        out_shape=jax.ShapeDtypeStruct((r, f), BF16),
        scratch_shapes=[pltpu.VMEM((tm, d), BF16)],
        compiler_params=_cparams(("arbitrary", "arbitrary")),
        name="ffn_up",
    )(x, g, mod, mod, w_up, w_up)


def _ffn_down_kernel(a_ref, w_ref, x_ref, gate_ref, gn_ref, o_ref):
    y = _dot(a_ref[...], w_ref[...])
    r = lax.rsqrt(jnp.mean(y * y, axis=-1, keepdims=True) + EPS)
    o_ref[...] = x_ref[...] + gate_ref[...] * ((y * r) * gn_ref[...])


def _ffn_down(act, x, mod, rows_per_seq, per_row, w_down, gn):
    r, d = x.shape
    f = act.shape[1]
    tm = _tile(rows_per_seq if not per_row else r, 256)
    return pl.pallas_call(
        _ffn_down_kernel,
        grid=(r // tm,),
        in_specs=[pl.BlockSpec((tm, f), lambda i: (i, 0)),
                  pl.BlockSpec((f, d), lambda i: (0, 0), pipeline_mode=pl.Buffered(1)),
                  pl.BlockSpec((tm, d), lambda i: (i, 0)),
                  _mod_spec(rows_per_seq, tm, d, 5, per_row),
                  pl.BlockSpec((1, d), lambda i: (0, 0))],
        out_specs=pl.BlockSpec((tm, d), lambda i: (i, 0)),
        out_shape=jax.ShapeDtypeStruct((r, d), F32),
        compiler_params=_cparams(("arbitrary",)),
        name="ffn_down",
    )(act, w_down, x, mod, gn)


def _group(x, mod, per_row, attn_fn, conv_state, s0, chunk, valid_len, pw, n_heads):
    b, l, d = x.shape
    r = b * l
    w = n_heads * LANES
    xf = x.reshape(r, d)
    oattn, ok, ov, ogdn, ozg, osm = _inproj(xf, mod, l, per_row, pw["g_mix_pre"], pw["w_all"], pw["w_gates"],
                                            pw["b_main"], pw["w_small"], pw["b_small"], w)
    a_out = attn_fn(oattn)
    q, k, v, gates = _gdn_prep(ogdn.reshape(b, l, 3 * w), osm.reshape(b, l, LANES), conv_state,
                               pw["conv_w"], pw["a_log"], pw["dt_bias"], n_heads, valid_len)
    g_out, s_new = _gdn(q, k, v, gates, ozg.reshape(b, l, -1), (2 * d) // w, pw["gdn_norm"], s0, chunk, n_heads)
    x1 = _merge(a_out, g_out.reshape(r, w), ozg, xf, mod, l, per_row, pw["w_pa"], pw["w_pg"], pw["w_o"],
                pw["g_mix_post"])
    act = _ffn_up(x1, mod, l, per_row, pw["g_ffn_pre"], pw["w_up"])
    x2 = _ffn_down(act, x1, mod, l, per_row, pw["w_down"], pw["g_ffn_post"])
    kv = (ok.reshape(b, l, n_heads, LANES), ov.reshape(b, l, n_heads, LANES))
    return x2.reshape(b, l, d), kv, ogdn, s_new


def kernel(x_prompt, x_sample, c_prompt, c_sample, cache_k, cache_v, page_table, state_ssm, state_conv, w_ada, b_ada, norm_mix_pre, norm_mix_post, norm_ffn_pre, norm_ffn_post, w_in, b_in, conv_w, lambda_q1, lambda_k1, lambda_q2, lambda_k2, attn_subln, a_log, dt_bias, gdn_norm, w_proj_attn, w_proj_gdn, w_out, w_ffn_up, w_ffn_down):
    bsz, seq, d = x_prompt.shape
    db, s_new, _ = x_sample.shape
    depth = w_in.shape[0]
    n_heads = cache_k.shape[3]
    hd = cache_k.shape[4]
    assert hd == LANES and state_ssm.shape[-1] == LANES and state_ssm.shape[-2] == LANES
    assert state_ssm.shape[2] == n_heads
    w = n_heads * hd
    taps = conv_w.shape[1]
    chunk_p = 64 if seq % 64 == 0 else seq
    slopes = 2.0 ** (-8.0 * jnp.arange(1, n_heads + 1, dtype=F32) / n_heads)
    assert (2 * d) % w == 0
    s_pad = -(-s_new // 8) * 8

    xp, xs = x_prompt, x_sample
    outs = [[] for _ in range(8)]
    for l in range(depth):
        lam_init = 0.8 - 0.6 * math.exp(-0.3 * l)
        lam_vecs = [t[l].reshape(1, -1) for t in (lambda_q1, lambda_k1, lambda_q2, lambda_k2)]
        subg = attn_subln[l].reshape(1, hd)

        c_all = jnp.concatenate([c_prompt, c_sample], axis=0)
        c_all = jnp.pad(c_all, ((0, (-c_all.shape[0]) % 8), (0, 0)))
        mod = _adaln(c_all, w_ada[l], b_ada[l])
        mod_p = mod[:bsz].reshape(bsz, 1, 6 * d)
        mod_s = jnp.repeat(mod[bsz:bsz + db], s_pad, axis=0)

        wt, bi = w_in[l].T, b_in[l]
        n6 = 6 * w
        n7 = 7 * w
        w_all = wt.astype(BF16)
        pw = dict(
            w_all=w_all, w_gates=w_all[n7 + 2 * n_heads:],
            b_main=jnp.concatenate([bi[:n6], bi[n7 + 2 * n_heads:], bi[n6:n7]]).reshape(1, -1),
            w_small=jnp.pad(w_all[n7:n7 + 2 * n_heads], ((0, LANES - 2 * n_heads), (0, 0))),
            b_small=jnp.pad(bi[n7:n7 + 2 * n_heads], (0, LANES - 2 * n_heads)).reshape(1, LANES),
            g_mix_pre=norm_mix_pre[l].reshape(1, d), g_mix_post=norm_mix_post[l].reshape(1, d),
            g_ffn_pre=norm_ffn_pre[l].reshape(1, d), g_ffn_post=norm_ffn_post[l].reshape(1, d),
            conv_w=conv_w[l], a_log=a_log[l], dt_bias=dt_bias[l], gdn_norm=gdn_norm[l].reshape(1, hd),
            w_pa=w_proj_attn[l].astype(BF16), w_pg=w_proj_gdn[l].astype(BF16), w_o=w_out[l].astype(BF16),
            w_up=w_ffn_up[l], w_down=w_ffn_down[l].astype(BF16),
        )

        def attn_p(oattn):
            o = _prompt_attention(oattn.reshape(bsz, seq, 3 * w), slopes, lam_vecs, subg, n_heads, lam_init)
            return o.reshape(bsz * seq, w)

        conv0 = jnp.zeros((bsz, taps - 1, 3 * w), F32)
        s0 = jnp.zeros((bsz, n_heads, hd, hd), F32)
        xp, kv_p, raw_p, ssm_p = _group(xp, mod_p, False, attn_p, conv0, s0, chunk_p, seq, pw, n_heads)

        xs_pad = jnp.pad(xs, ((0, 0), (0, s_pad - s_new), (0, 0)))

        def attn_s(oattn):
            qkv = oattn.reshape(db, s_pad, 3, n_heads, hd)[:, :s_new]
            qh = qkv[:, :, 0].transpose(0, 2, 1, 3)
            qrows = jax.vmap(lambda t: _split_maps(t, (hd // 2) ** -0.5))(qh.reshape(db, n_heads * s_new, hd))
            knew = qkv[:, :, 1].reshape(db, s_new * n_heads, hd)
            vnew = qkv[:, :, 2].reshape(db, s_new * n_heads, hd)
            o = _sample_attention(qrows, knew, vnew, cache_k, cache_v, l, page_table, slopes, lam_vecs, subg,
                                  lam_init)
            o = o.reshape(db, n_heads, s_new, hd).transpose(0, 2, 1, 3).reshape(db, s_new, w)
            return jnp.pad(o, ((0, 0), (0, s_pad - s_new), (0, 0))).reshape(db * s_pad, w).astype(BF16)

        xs_full, kv_s, raw_s, ssm_s = _group(xs_pad, mod_s, True, attn_s, state_conv[l], state_ssm[l], s_pad,
                                             s_new, pw, n_heads)
        xs = xs_full[:, :s_new]

        raw_p = raw_p.reshape(bsz, seq, 3 * w)
        raw_s = jnp.concatenate([state_conv[l], raw_s.reshape(db, s_pad, 3 * w)[:, :s_new]], axis=1)
        for lst, val in zip(outs, (kv_p[0], kv_p[1], kv_s[0][:, :s_new], kv_s[1][:, :s_new], ssm_p, ssm_s,
                                   raw_p[:, seq - (taps - 1):], raw_s[:, s_new:])):
            lst.append(val)

    return (xp, xs) + tuple(jnp.stack(o) for o in outs)
```

```python
import functools
import math

import jax
import jax.numpy as jnp
from jax import lax
from jax.experimental import pallas as pl
from jax.experimental.pallas import tpu as pltpu

F32 = jnp.float32
BF16 = jnp.bfloat16
EPS = 1e-6
NEG = -1e30
LANES = 128
VMEM_LIMIT = 60 * 1024 * 1024
HI = lax.Precision.HIGHEST
LOG2E = math.log2(math.e)


def _cparams(sem):
    return pltpu.CompilerParams(dimension_semantics=sem, vmem_limit_bytes=VMEM_LIMIT)


def _tile(n, target, mult=8):
    if n <= target:
        return n
    for t in range(target, 0, -1):
        if n % t == 0 and t % mult == 0:
            return t
    return n


def _silu(x):
    return x * jax.nn.sigmoid(x)


def _dot(a, b):
    return jnp.dot(a, b, preferred_element_type=F32)


def _split(a):
    hi = a.astype(BF16)
    return hi, (a - hi.astype(F32)).astype(BF16)


def _dot3(a, b):
    ah, al = _split(a)
    bh, bl = _split(b)
    return _dot(ah, bh) + (_dot(ah, bl) + _dot(al, bh))


def _dot_nt(a, b, precision=None):
    return lax.dot_general(a, b, (((1,), (1,)), ((), ())), preferred_element_type=F32,
                           precision=precision)


def _dot_tn(a, b):
    return lax.dot_general(a, b, (((0,), (0,)), ((), ())), preferred_element_type=F32)


def _adaln_kernel(c_ref, w_ref, b_ref, o_ref):
    a = _silu(c_ref[...]).astype(BF16)
    o_ref[...] = _dot(a, w_ref[...].astype(BF16)) + b_ref[...]


def _adaln(c, w, b):
    m, d = c.shape
    n = w.shape[1]
    tn = _tile(n, 1024, LANES)
    return pl.pallas_call(
        _adaln_kernel,
        grid=(n // tn,),
        in_specs=[pl.BlockSpec((m, d), lambda j: (0, 0)),
                  pl.BlockSpec((d, tn), lambda j: (0, j)),
                  pl.BlockSpec((1, tn), lambda j: (0, j))],
        out_specs=pl.BlockSpec((m, tn), lambda j: (0, j)),
        out_shape=jax.ShapeDtypeStruct((m, n), F32),
        compiler_params=_cparams(("arbitrary",)),
        name="adaln",
    )(c, w, b.reshape(1, n))


def _modulated(x, g, shift, scale):
    r = lax.rsqrt(jnp.mean(x * x, axis=-1, keepdims=True) + EPS)
    return ((x * r) * g) * (1.0 + scale) + shift


def _inproj_kernel(x_ref, g_ref, sh_ref, sc_ref, wa_ref, wg_ref, b_ref, ws_ref, bs_ref,
                   oattn_ref, ok_ref, ov_ref, ogdn_ref, ozg_ref, osm_ref, h_ref, *, spw):
    j = pl.program_id(1)
    nj = pl.num_programs(1)
    hps = ok_ref.shape[1] // spw

    @pl.when(j == 0)
    def _():
        hb = _modulated(x_ref[...], g_ref[...], sh_ref[...], sc_ref[...]).astype(BF16)
        h_ref[...] = hb
        osm_ref[...] = _dot_nt(hb, ws_ref[...]) + bs_ref[...]

    def project(w_ref):
        return _dot_nt(h_ref[...], w_ref[...].astype(BF16)) + b_ref[...]

    def heads_to_sublanes(y, o_ref, first):
        o_ref[:, first:first + hps, :] = pltpu.einshape("t(hd)->thd", y, h=hps)

    @pl.when(j < 3 * spw)
    def _():
        y = project(wa_ref)
        oattn_ref[...] = y.astype(BF16)
        for part in range(spw):
            @pl.when(j == spw + part)
            def _():
                heads_to_sublanes(y, ok_ref, part * hps)

            @pl.when(j == 2 * spw + part)
            def _():
                heads_to_sublanes(y, ov_ref, part * hps)

    @pl.when((j >= 3 * spw) & (j < 6 * spw))
    def _():
        ogdn_ref[...] = project(wa_ref)

    @pl.when((j >= 6 * spw) & (j < nj - spw))
    def _():
        ozg_ref[...] = project(wg_ref).astype(BF16)

    @pl.when(j >= nj - spw)
    def _():
        ozg_ref[...] = project(wa_ref).astype(BF16)


def _mod_spec(rows_per_seq, tm, d, chunk, per_row):
    if per_row:
        return pl.BlockSpec((tm, d), lambda i, *_: (i, chunk))
    return pl.BlockSpec((None, 1, d), lambda i, *_: ((i * tm) // rows_per_seq, 0, chunk))


def _inproj(x, mod, rows_per_seq, per_row, g, w_all, w_gates, b_main, w_small, b_small, width):
    r, d = x.shape
    n_heads = width // LANES
    tm = _tile(rows_per_seq if not per_row else r, 1024)
    spw = 2 if (n_heads % 2 == 0 and tm > 512) else 1
    tn = width // spw
    n_gate = w_gates.shape[0] // tn
    nj = 7 * spw + n_gate
    grid = (r // tm, nj)
    clip = lambda j, lo, hi: jnp.minimum(jnp.maximum(j - lo, 0), hi - lo)
    return pl.pallas_call(
        functools.partial(_inproj_kernel, spw=spw),
        grid=grid,
        in_specs=[pl.BlockSpec((tm, d), lambda i, j: (i, 0), pipeline_mode=pl.Buffered(1)),
                  pl.BlockSpec((1, d), lambda i, j: (0, 0)),
                  _mod_spec(rows_per_seq, tm, d, 0, per_row),
                  _mod_spec(rows_per_seq, tm, d, 1, per_row),
                  pl.BlockSpec((tn, d), lambda i, j: (jnp.where(j < 6 * spw, j, clip(j, nj - spw, nj - 1) + 6 * spw), 0)),
                  pl.BlockSpec((tn, d), lambda i, j: (clip(j, 6 * spw, 6 * spw + n_gate - 1), 0)),
                  pl.BlockSpec((1, tn), lambda i, j: (0, j)),
                  pl.BlockSpec((LANES, d), lambda i, j: (0, 0)),
                  pl.BlockSpec((1, LANES), lambda i, j: (0, 0))],
        out_specs=[pl.BlockSpec((tm, tn), lambda i, j: (i, clip(j, 0, 3 * spw - 1))),
                   pl.BlockSpec((tm, n_heads, LANES), lambda i, j: (i, 0, 0)),
                   pl.BlockSpec((tm, n_heads, LANES), lambda i, j: (i, 0, 0)),
                   pl.BlockSpec((tm, tn), lambda i, j: (i, clip(j, 3 * spw, 6 * spw - 1))),
                   pl.BlockSpec((tm, tn), lambda i, j: (i, clip(j, 6 * spw, nj - 1))),
                   pl.BlockSpec((tm, LANES), lambda i, j: (i, 0))],
        out_shape=[jax.ShapeDtypeStruct((r, 3 * width), BF16),
                   jax.ShapeDtypeStruct((r, n_heads, LANES), F32),
                   jax.ShapeDtypeStruct((r, n_heads, LANES), F32),
                   jax.ShapeDtypeStruct((r, 3 * width), F32),
                   jax.ShapeDtypeStruct((r, (nj - 6 * spw) * tn), BF16),
                   jax.ShapeDtypeStruct((r, LANES), F32)],
        scratch_shapes=[pltpu.VMEM((tm, d), BF16)],
        compiler_params=_cparams(("arbitrary", "arbitrary")),
        name="inproj",
    )(x, g, mod, mod, w_all, w_gates, b_main, w_small, b_small)


def _lambda_value(lq1, lk1, lq2, lk2, lam_init):
    s1 = jnp.sum(lq1 * lk1, axis=-1, keepdims=True)
    s2 = jnp.sum(lq2 * lk2, axis=-1, keepdims=True)
    return jnp.exp(s1) - jnp.exp(s2) + lam_init


def _diff_finish(acc, l, lam, subg, lam_init, half):
    o1 = acc[:half] / l[:half]
    o2 = acc[half:] / l[half:]
    a = o1 - lam * o2
    r = lax.rsqrt(jnp.mean(a * a, axis=-1, keepdims=True) + EPS)
    return ((a * r) * subg) * (1.0 - lam_init)


def _split_maps(q, scale):
    lane = lax.broadcasted_iota(jnp.int32, q.shape, 1)
    qs = (q.astype(F32) * scale).astype(q.dtype)
    zero = jnp.zeros_like(qs)
    half = q.shape[-1] // 2
    return jnp.concatenate([jnp.where(lane < half, qs, zero), jnp.where(lane >= half, qs, zero)], axis=0)


ONES_ROWS = 16


def _pattn_kernel(slopes_ref, lq1_ref, lk1_ref, lq2_ref, lk2_ref, subg_ref, q_ref, k_ref, v_ref,
                  o_ref, vt_ref, bias_ref, m_ref, acc_ref, sa_ref, sb_ref, *, tq, hpb, lam_init, scale):
    hg = pl.program_id(1)
    qi = pl.program_id(2)
    seq = k_ref.shape[0]
    heads = range(hpb)
    lanes = [slice(e * LANES, (e + 1) * LANES) for e in heads]
    slope = [slopes_ref[hg * hpb + e] * LOG2E for e in heads]

    @pl.when(qi == 0)
    def _():
        for e in heads:
            for c in range(seq // tq):
                blk = v_ref[c * tq:(c + 1) * tq, lanes[e]].astype(F32)
                vt_ref[e, 0:LANES, c * tq:(c + 1) * tq] = blk.T.astype(BF16)
            vt_ref[e, LANES:, :] = jnp.ones((ONES_ROWS, seq), BF16)
            bias_ref[e] = slope[e] * lax.broadcasted_iota(jnp.int32, bias_ref.shape[1:], 0).astype(F32)

    qq = [_split_maps(q_ref[:, lanes[e]], scale) for e in heads]
    m_ref[...] = jnp.full(m_ref.shape, NEG, F32)
    acc_ref[...] = jnp.zeros(acc_ref.shape, F32)

    def scores(c, dst_ref):
        start = pl.multiple_of(c * tq, tq)
        for e in heads:
            dst_ref[e] = _dot_nt(k_ref[pl.ds(start, tq), lanes[e]], qq[e]) + bias_ref[e]

    def chunk(c, src_ref, masked):
        start = pl.multiple_of(c * tq, tq)
        for e in heads:
            st = src_ref[e]
            if masked:
                kr = lax.broadcasted_iota(jnp.int32, st.shape, 0)
                qc = lax.broadcasted_iota(jnp.int32, st.shape, 1)
                qc = jnp.where(qc >= tq, qc - tq, qc)
                st = jnp.where(qc >= kr, st, NEG)
            m_prev = m_ref[e] - slope[e] * tq
            m_new = jnp.maximum(m_prev, jnp.max(st, axis=0, keepdims=True))
            alpha = jnp.exp2(m_prev - m_new)
            p = jnp.exp2(st - m_new).astype(BF16)
            acc_ref[e] = _dot(vt_ref[e, :, pl.ds(start, tq)], p) + alpha * acc_ref[e]
            m_ref[e] = m_new

    scores(0, sa_ref)

    def pair(i, carry):
        c = 2 * i
        scores(c + 1, sb_ref)
        chunk(c, sa_ref, False)
        scores(c + 2, sa_ref)
        chunk(c + 1, sb_ref, False)
        return carry

    lax.fori_loop(0, qi // 2, pair, 0)

    @pl.when(qi % 2 == 1)
    def _():
        scores(qi, sb_ref)
        chunk(qi - 1, sa_ref, False)
        chunk(qi, sb_ref, True)

    @pl.when(qi % 2 == 0)
    def _():
        chunk(qi, sa_ref, True)

    lam = _lambda_value(lq1_ref[...], lk1_ref[...], lq2_ref[...], lk2_ref[...], lam_init)
    for e in heads:
        acc = acc_ref[e]
        on = acc[:LANES] / acc[LANES:LANES + 1]
        a = (on[:, :tq] - lam * on[:, tq:]).T
        r = lax.rsqrt(jnp.mean(a * a, axis=-1, keepdims=True) + EPS)
        o_ref[:, lanes[e]] = (((a * r) * subg_ref[...]) * (1.0 - lam_init)).astype(o_ref.dtype)


def _prompt_attention(qkv, slopes, lam_vecs, subg, n_heads, lam_init):
    b, l, _ = qkv.shape
    tq = _tile(l, 256)
    hd = LANES
    hpb = 2 if n_heads % 2 == 0 else 1
    ng = n_heads // hpb
    vec = lambda n: pl.BlockSpec((1, n), lambda b_, h, i: (0, 0))
    kern = functools.partial(_pattn_kernel, tq=tq, hpb=hpb, lam_init=lam_init, scale=(hd // 2) ** -0.5 * LOG2E)
    return pl.pallas_call(
        kern,
        grid=(b, ng, l // tq),
        in_specs=[pl.BlockSpec(memory_space=pltpu.SMEM),
                  vec(hd // 2), vec(hd // 2), vec(hd // 2), vec(hd // 2), vec(hd),
                  pl.BlockSpec((None, tq, hpb * hd), lambda b_, h, i: (b_, i, h)),
                  pl.BlockSpec((None, l, hpb * hd), lambda b_, h, i: (b_, 0, ng + h)),
                  pl.BlockSpec((None, l, hpb * hd), lambda b_, h, i: (b_, 0, 2 * ng + h))],
        out_specs=pl.BlockSpec((None, tq, hpb * hd), lambda b_, h, i: (b_, i, h)),
        out_shape=jax.ShapeDtypeStruct((b, l, n_heads * hd), BF16),
        scratch_shapes=[pltpu.VMEM((hpb, hd + ONES_ROWS, l), BF16), pltpu.VMEM((hpb, tq, 2 * tq), F32),
                        pltpu.VMEM((hpb, 1, 2 * tq), F32), pltpu.VMEM((hpb, hd + ONES_ROWS, 2 * tq), F32),
                        pltpu.VMEM((hpb, tq, 2 * tq), F32), pltpu.VMEM((hpb, tq, 2 * tq), F32)],
        compiler_params=_cparams(("arbitrary", "arbitrary", "arbitrary")),
        name="prompt_attn",
    )(slopes, *lam_vecs, subg, qkv, qkv, qkv)


PAGE_RING = 3


def _sattn_kernel(pt_ref, slope_ref, bias_ref, biasn_ref, lq1_ref, lk1_ref, lq2_ref, lk2_ref, subg_ref,
                  q_ref, kn_ref, vn_ref, ck_hbm, cv_hbm, o_ref, kbuf, vbuf, sem, m_ref, l_ref, acc_ref,
                  *, layer, n_pages_step, n_seq, ng, page, lam_init):
    b = pl.program_id(0)
    g = pl.program_id(1)
    n = b * ng + g
    total = n_seq * ng

    def group_copies(seq, grp, slot):
        copies = []
        for i in range(n_pages_step):
            pid = pt_ref[seq, grp * n_pages_step + i]
            copies.append(pltpu.make_async_copy(ck_hbm.at[layer, pid], kbuf.at[slot, i], sem.at[0, slot]))
            copies.append(pltpu.make_async_copy(cv_hbm.at[layer, pid], vbuf.at[slot, i], sem.at[1, slot]))
        return copies

    @pl.when(n == 0)
    def _():
        for d in range(min(PAGE_RING - 1, total)):
            for cp in group_copies(d // ng, d % ng, d):
                cp.start()

    nxt = n + (PAGE_RING - 1)

    @pl.when(nxt < total)
    def _():
        for cp in group_copies(lax.div(nxt, ng), lax.rem(nxt, ng), lax.rem(nxt, PAGE_RING)):
            cp.start()

    slot = lax.rem(n, PAGE_RING)
    for cp in group_copies(b, g, slot):
        cp.wait()
    k_refs = [kbuf.at[slot, i] for i in range(n_pages_step)]
    v_refs = [vbuf.at[slot, i] for i in range(n_pages_step)]

    @pl.when(g == 0)
    def _():
        m_ref[...] = jnp.full(m_ref.shape, NEG, F32)
        l_ref[...] = jnp.zeros(l_ref.shape, F32)
        acc_ref[...] = jnp.zeros(acc_ref.shape, F32)

    qq = q_ref[...]
    step = slope_ref[...] * (page * n_pages_step)

    def update(s, vs):
        m_prev = m_ref[...] - step
        m_new = jnp.maximum(m_prev, jnp.max(s, axis=-1, keepdims=True))
        alpha = jnp.exp(m_prev - m_new)
        p = jnp.exp(s - m_new)
        l_ref[...] = alpha * l_ref[...] + jnp.sum(p, axis=-1, keepdims=True)
        pb = p.astype(BF16)
        pv = None
        off = 0
        for vf in vs:
            part = _dot(pb[:, off:off + vf.shape[0]], vf)
            pv = part if pv is None else pv + part
            off += vf.shape[0]
        acc_ref[...] = alpha * acc_ref[...] + pv
        m_ref[...] = m_new

    s = jnp.concatenate([_dot_nt(qq, k_refs[i][...].reshape(-1, LANES).astype(BF16))
                         for i in range(n_pages_step)], axis=1) + bias_ref[...]
    update(s, [v_refs[i][...].reshape(-1, LANES).astype(BF16) for i in range(n_pages_step)])

    @pl.when(g == ng - 1)
    def _():
        update(_dot_nt(qq, kn_ref[...]) + biasn_ref[...], [vn_ref[...]])
        lam = _lambda_value(lq1_ref[...], lk1_ref[...], lq2_ref[...], lk2_ref[...], lam_init)
        half = acc_ref.shape[0] // 2
        o_ref[...] = _diff_finish(acc_ref[...], l_ref[...], lam, subg_ref[...], lam_init, half)


def _sample_attention(qrows, knew, vnew, cache_k, cache_v, layer, page_table, slopes, lam_vecs, subg, lam_init):
    db, nrow, hd = qrows.shape
    _, n_phys, page, n_heads, _ = cache_k.shape
    s_new = knew.shape[1] // n_heads
    n_pages = page_table.shape[1]
    pps = 8 if n_pages % 8 == 0 else 1
    r = jnp.arange(nrow)
    r_head = (r // s_new) % n_heads
    r_tok = r % s_new
    slope_rows = slopes[r_head][:, None]
    c = jnp.arange(pps * page * n_heads)
    bias = jnp.where(r_head[:, None] == (c % n_heads)[None, :], slope_rows * (c // n_heads)[None, :].astype(F32), NEG)
    cn = jnp.arange(s_new * n_heads)
    ok = (r_head[:, None] == (cn % n_heads)[None, :]) & ((cn // n_heads)[None, :] <= r_tok[:, None])
    biasn = jnp.where(ok, slope_rows * (cn // n_heads)[None, :].astype(F32), NEG)

    const = lambda shape: pl.BlockSpec(shape, lambda b, g, pt: (0,) * len(shape))
    kern = functools.partial(_sattn_kernel, layer=layer, n_pages_step=pps, n_seq=db, ng=n_pages // pps,
                             page=float(page), lam_init=lam_init)
    ring = (PAGE_RING, pps, page, n_heads, hd)
    grid_spec = pltpu.PrefetchScalarGridSpec(
        num_scalar_prefetch=1,
        grid=(db, n_pages // pps),
        in_specs=[const((nrow, 1)), const((nrow, pps * page * n_heads)), const((nrow, s_new * n_heads)),
                  const((1, hd // 2)), const((1, hd // 2)), const((1, hd // 2)), const((1, hd // 2)), const((1, hd)),
                  pl.BlockSpec((None, nrow, hd), lambda b, g, pt: (b, 0, 0)),
                  pl.BlockSpec((None, s_new * n_heads, hd), lambda b, g, pt: (b, 0, 0)),
                  pl.BlockSpec((None, s_new * n_heads, hd), lambda b, g, pt: (b, 0, 0)),
                  pl.BlockSpec(memory_space=pl.ANY), pl.BlockSpec(memory_space=pl.ANY)],
        out_specs=pl.BlockSpec((None, nrow // 2, hd), lambda b, g, pt: (b, 0, 0)),
        scratch_shapes=[pltpu.VMEM(ring, F32), pltpu.VMEM(ring, F32), pltpu.SemaphoreType.DMA((2, PAGE_RING)),
                        pltpu.VMEM((nrow, 1), F32), pltpu.VMEM((nrow, 1), F32), pltpu.VMEM((nrow, hd), F32)],
    )
    return pl.pallas_call(
        kern,
        grid_spec=grid_spec,
        out_shape=jax.ShapeDtypeStruct((db, nrow // 2, hd), F32),
        compiler_params=_cparams(("arbitrary", "arbitrary")),
        name="sample_attn",
    )(page_table, slope_rows, bias, biasn, *lam_vecs, subg, qrows, knew, vnew, cache_k, cache_v)


def _gdn_prep_kernel(x_ref, sm_ref, st_ref, cw_ref, alog_ref, dtb_ref, q_ref, k_ref, v_ref, gate_ref, pad_ref,
                     *, tt, n_heads, valid_len, taps):
    t = pl.program_id(1)
    halo = 8
    w3 = n_heads * LANES

    @pl.when(t == 0)
    def _():
        pad_ref[0:halo, :] = jnp.zeros((halo, pad_ref.shape[1]), F32)
        pad_ref[halo - (taps - 1):halo, :] = st_ref[...]

    pad_ref[halo:halo + tt, :] = x_ref[...]
    y = pad_ref[halo:halo + tt, :] * cw_ref[taps - 1:taps, :]
    for j in range(taps - 1):
        off = halo - (taps - 1) + j
        y = y + pad_ref[off:off + tt, :] * cw_ref[j:j + 1, :]
    y = _silu(y)
    pad_ref[0:halo, :] = pad_ref[tt:tt + halo, :]

    def l2n(z):
        return z * lax.rsqrt(jnp.sum(z * z, axis=-1, keepdims=True) + EPS)

    for hh in range(n_heads):
        sl = slice(hh * LANES, (hh + 1) * LANES)
        q_ref[:, sl] = l2n(y[:, sl]) * (LANES ** -0.5)
        k_ref[:, sl] = l2n(y[:, w3 + hh * LANES:w3 + (hh + 1) * LANES])
    v_ref[...] = y[:, 2 * w3:]

    x = sm_ref[...]
    lane = lax.broadcasted_iota(jnp.int32, x.shape, 1)
    row = lax.broadcasted_iota(jnp.int32, x.shape, 0) + t * tt
    beta = jax.nn.sigmoid(x)
    z = x + dtb_ref[...]
    softplus = jnp.maximum(z, 0.0) + jnp.log(1.0 + jnp.exp(-jnp.abs(z)))
    gdec = -jnp.exp(alog_ref[...]) * softplus
    out = jnp.where(lane < n_heads, beta, gdec)
    gate_ref[...] = jnp.where((row < valid_len) & (lane < 2 * n_heads), out, 0.0)


def _gdn_prep(x, small, state, conv_w, a_log, dt_bias, n_heads, valid_len):
    b, l, ch = x.shape
    taps = conv_w.shape[0]
    tt = _tile(l, 256)
    w = n_heads * LANES
    lanes = jnp.arange(LANES)
    in_g = (lanes >= n_heads) & (lanes < 2 * n_heads)
    alog_v = jnp.where(in_g, a_log[jnp.clip(lanes - n_heads, 0, n_heads - 1)], 0.0).reshape(1, LANES)
    dtb_v = jnp.where(in_g, dt_bias[jnp.clip(lanes - n_heads, 0, n_heads - 1)], 0.0).reshape(1, LANES)
    kern = functools.partial(_gdn_prep_kernel, tt=tt, n_heads=n_heads, valid_len=valid_len, taps=taps)
    row_spec = lambda n: pl.BlockSpec((None, tt, n), lambda b_, t: (b_, t, 0))
    return pl.pallas_call(
        kern,
        grid=(b, l // tt),
        in_specs=[row_spec(ch), row_spec(LANES),
                  pl.BlockSpec((None, taps - 1, ch), lambda b_, t: (b_, 0, 0)),
                  pl.BlockSpec((taps, ch), lambda b_, t: (0, 0)),
                  pl.BlockSpec((1, LANES), lambda b_, t: (0, 0)),
                  pl.BlockSpec((1, LANES), lambda b_, t: (0, 0))],
        out_specs=[row_spec(w), row_spec(w), row_spec(w), row_spec(LANES)],
        out_shape=[jax.ShapeDtypeStruct((b, l, w), F32)] * 3 + [jax.ShapeDtypeStruct((b, l, LANES), F32)],
        scratch_shapes=[pltpu.VMEM((tt + 8, ch), F32)],
        compiler_params=_cparams(("arbitrary", "arbitrary")),
        name="gdn_prep",
    )(x, small, state, conv_w, alog_v, dtb_v)


def _gdn_kernel(q_ref, k_ref, v_ref, gate_ref, z_ref, gn_ref, s0_ref, o_ref, s_ref, *, chunk, n_chunks, n_heads):
    t = pl.program_id(1)
    nb = q_ref.shape[0]

    @pl.when(t == 0)
    def _():
        s_ref[...] = s0_ref[...]

    c_ = chunk
    ri = lax.broadcasted_iota(jnp.int32, (c_, c_), 0)
    ci = lax.broadcasted_iota(jnp.int32, (c_, c_), 1)
    causal = ri >= ci
    strict = ri > ci
    tri = jnp.where(causal, 1.0, 0.0).astype(F32)
    eye = jnp.where(ri == ci, 1.0, 0.0).astype(F32)
    n_sq = max(int(math.log2(c_)) - 1, 0)
    gn = gn_ref[...]
    sel_r = lax.broadcasted_iota(jnp.int32, (8, LANES), 0)
    sel_c = lax.broadcasted_iota(jnp.int32, (8, LANES), 1)
    sel = jnp.where(sel_c == sel_r + n_heads, 1.0, 0.0).astype(F32)
    zpad = jnp.zeros((LANES - c_, LANES), F32)

    def one_chunk(c, carry):
        rows = pl.ds(c * c_ if n_chunks == 1 else pl.multiple_of(c * c_, c_), c_)
        gates = [gate_ref[bb, rows, :] for bb in range(nb)]
        gcum = [jnp.dot(tri, gates[bb], preferred_element_type=F32, precision=HI) for bb in range(nb)]
        gcum_t = [_dot_nt(sel, gcum[bb], precision=HI) for bb in range(nb)]
        hs = range(nb * n_heads)
        seq = [u_ // n_heads for u_ in hs]
        head = [u_ % n_heads for u_ in hs]
        sl = [(seq[u_], rows, slice(head[u_] * LANES, (head[u_] + 1) * LANES)) for u_ in hs]
        q = [q_ref[sl[hh]] for hh in hs]
        k = [k_ref[sl[hh]] for hh in hs]
        beta = [gates[seq[hh]][:, head[hh]:head[hh] + 1] for hh in hs]
        gc = [gcum[seq[hh]][:, n_heads + head[hh]:n_heads + head[hh] + 1] for hh in hs]
        glast = [gcum[seq[hh]][c_ - 1:c_, n_heads + head[hh]:n_heads + head[hh] + 1] for hh in hs]
        decay = [jnp.exp(jnp.where(causal, gc[hh] - gcum_t[seq[hh]][head[hh]:head[hh] + 1, :], NEG)) for hh in hs]
        kb = [k[hh].astype(BF16) for hh in hs]
        qk_kk = [_dot_nt(jnp.concatenate([kb[hh], q[hh].astype(BF16)], axis=0), kb[hh]) for hh in hs]
        a = [jnp.where(strict, beta[hh] * qk_kk[hh][:c_] * decay[hh], 0.0) for hh in hs]
        tinv = [eye - a[hh] for hh in hs]
        ap = a
        for _ in range(n_sq):
            ap = [_dot3(ap[hh], ap[hh]) for hh in hs]
            tinv = [tinv[hh] + _dot3(tinv[hh], ap[hh]) for hh in hs]
        eg = [jnp.exp(gc[hh]) for hh in hs]
        rhs = [jnp.concatenate([v_ref[sl[hh]] * beta[hh], k[hh] * (beta[hh] * eg[hh])], axis=1).astype(BF16)
               for hh in hs]
        uw = [_dot(tinv[hh].astype(BF16), rhs[hh]) for hh in hs]
        s = [s_ref[seq[hh], head[hh]] for hh in hs]
        wq = [_dot(jnp.concatenate([uw[hh][:, LANES:], q[hh] * eg[hh]], axis=0).astype(BF16), s[hh].astype(BF16))
              for hh in hs]
        u = [uw[hh][:, :LANES] - wq[hh][:c_] for hh in hs]
        ub = [u[hh].astype(BF16) for hh in hs]
        o = [wq[hh][c_:] + _dot((qk_kk[hh][c_:] * decay[hh]).astype(BF16), ub[hh]) for hh in hs]
        for hh in hs:
            kd = k[hh] * jnp.exp(glast[hh] - gc[hh])
            kd_t = jnp.concatenate([kd, zpad], axis=0).T.astype(BF16)
            u_pad = jnp.concatenate([u[hh], zpad], axis=0).astype(BF16)
            s_ref[seq[hh], head[hh]] = s[hh] * jnp.exp(glast[hh]) + _dot(kd_t, u_pad)
        for hh in hs:
            r = lax.rsqrt(jnp.mean(o[hh] * o[hh], axis=-1, keepdims=True) + EPS)
            o_ref[sl[hh]] = (((o[hh] * r) * gn) * _silu(z_ref[sl[hh]].astype(F32))).astype(o_ref.dtype)
        return carry

    if n_chunks == 1:
        one_chunk(0, 0)
    else:
        lax.fori_loop(0, n_chunks, one_chunk, 0)


def _gdn(q, k, v, gates, z_arr, z_block, gn, s0, chunk, n_heads):
    b, l, w = q.shape
    tt = _tile(l, max(chunk, 256), chunk)
    nb = 2 if b % 2 == 0 else 1
    kern = functools.partial(_gdn_kernel, chunk=chunk, n_chunks=tt // chunk, n_heads=n_heads)
    row_spec = lambda n: pl.BlockSpec((nb, tt, n), lambda b_, t: (b_, t, 0))
    st_spec = pl.BlockSpec((nb, n_heads, LANES, LANES), lambda b_, t: (b_, 0, 0, 0))
    return pl.pallas_call(
        kern,
        grid=(b // nb, l // tt),
        in_specs=[row_spec(w), row_spec(w), row_spec(w), row_spec(LANES),
                  pl.BlockSpec((nb, tt, w), lambda b_, t: (b_, t, z_block)),
                  pl.BlockSpec((1, LANES), lambda b_, t: (0, 0)),
                  st_spec],
        out_specs=[row_spec(w), st_spec],
        out_shape=[jax.ShapeDtypeStruct((b, l, w), BF16),
                   jax.ShapeDtypeStruct(s0.shape, F32)],
        compiler_params=_cparams(("arbitrary", "arbitrary")),
        name="gdn",
    )(q, k, v, gates, z_arr, gn, s0)


def _merge_kernel(a_ref, g_ref, ga_ref, gg_ref, wpa_ref, wpg_ref, wo_ref, x_ref, gate_ref, gn_ref, o_ref):
    pa = _dot(a_ref[...], wpa_ref[...])
    pg = _dot(g_ref[...], wpg_ref[...])
    m = jax.nn.sigmoid(ga_ref[...].astype(F32)) * pa + jax.nn.sigmoid(gg_ref[...].astype(F32)) * pg
    y = _dot(m.astype(BF16), wo_ref[...])
    r = lax.rsqrt(jnp.mean(y * y, axis=-1, keepdims=True) + EPS)
    o_ref[...] = x_ref[...] + gate_ref[...] * ((y * r) * gn_ref[...])


def _merge(a_out, g_out, zg, x, mod, rows_per_seq, per_row, w_pa, w_pg, w_o, gn):
    r, d = x.shape
    w = a_out.shape[1]
    tm = _tile(rows_per_seq if not per_row else r, 256)
    resident = lambda shape: pl.BlockSpec(shape, lambda i: (0, 0), pipeline_mode=pl.Buffered(1))
    return pl.pallas_call(
        _merge_kernel,
        grid=(r // tm,),
        in_specs=[pl.BlockSpec((tm, w), lambda i: (i, 0)),
                  pl.BlockSpec((tm, w), lambda i: (i, 0)),
                  pl.BlockSpec((tm, d), lambda i: (i, 0)),
                  pl.BlockSpec((tm, d), lambda i: (i, 1)),
                  resident((w, d)), resident((w, d)), resident((d, d)),
                  pl.BlockSpec((tm, d), lambda i: (i, 0)),
                  _mod_spec(rows_per_seq, tm, d, 2, per_row),
                  pl.BlockSpec((1, d), lambda i: (0, 0))],
        out_specs=pl.BlockSpec((tm, d), lambda i: (i, 0)),
        out_shape=jax.ShapeDtypeStruct((r, d), F32),
        compiler_params=_cparams(("arbitrary",)),
        name="merge",
    )(a_out, g_out, zg, zg, w_pa, w_pg, w_o, x, mod, gn)


def _ffn_up_kernel(x_ref, g_ref, sh_ref, sc_ref, wg_ref, wu_ref, o_ref, h_ref):
    @pl.when(pl.program_id(1) == 0)
    def _():
        h_ref[...] = _modulated(x_ref[...], g_ref[...], sh_ref[...], sc_ref[...]).astype(BF16)

    h = h_ref[...]
    gt = _dot(h, wg_ref[...].astype(BF16))
    up = _dot(h, wu_ref[...].astype(BF16))
    o_ref[...] = (_silu(gt) * up).astype(o_ref.dtype)


def _ffn_up(x, mod, rows_per_seq, per_row, g, w_up):
    r, d = x.shape
    f = w_up.shape[1] // 2
    tn = _tile(f, 512, LANES)
    nj = f // tn
    tm = _tile(rows_per_seq if not per_row else r, 1024)
    return pl.pallas_call(
        _ffn_up_kernel,
        grid=(r // tm, nj),
        in_specs=[pl.BlockSpec((tm, d), lambda i, j: (i, 0)),
                  pl.BlockSpec((1, d), lambda i, j: (0, 0)),
                  _mod_spec(rows_per_seq, tm, d, 3, per_row),
                  _mod_spec(rows_per_seq, tm, d, 4, per_row),
                  pl.BlockSpec((d, tn), lambda i, j: (0, j)),
                  pl.BlockSpec((d, tn), lambda i, j: (0, nj + j))],
        out_specs=pl.BlockSpec((tm, tn), lambda i, j: (i, j)),
        out_shape=jax.ShapeDtypeStruct((r, f), BF16),
        scratch_shapes=[pltpu.VMEM((tm, d), BF16)],
        compiler_params=_cparams(("arbitrary", "arbitrary")),
        name="ffn_up",
    )(x, g, mod, mod, w_up, w_up)


def _ffn_down_kernel(a_ref, w_ref, x_ref, gate_ref, gn_ref, o_ref):
    y = _dot(a_ref[...], w_ref[...])
    r = lax.rsqrt(jnp.mean(y * y, axis=-1, keepdims=True) + EPS)
    o_ref[...] = x_ref[...] + gate_ref[...] * ((y * r) * gn_ref[...])


def _ffn_down(act, x, mod, rows_per_seq, per_row, w_down, gn):
    r, d = x.shape
    f = act.shape[1]
    tm = _tile(rows_per_seq if not per_row else r, 256)
    return pl.pallas_call(
        _ffn_down_kernel,
        grid=(r // tm,),
        in_specs=[pl.BlockSpec((tm, f), lambda i: (i, 0)),
                  pl.BlockSpec((f, d), lambda i: (0, 0), pipeline_mode=pl.Buffered(1)),
                  pl.BlockSpec((tm, d), lambda i: (i, 0)),
                  _mod_spec(rows_per_seq, tm, d, 5, per_row),
                  pl.BlockSpec((1, d), lambda i: (0, 0))],
        out_specs=pl.BlockSpec((tm, d), lambda i: (i, 0)),
        out_shape=jax.ShapeDtypeStruct((r, d), F32),
        compiler_params=_cparams(("arbitrary",)),
        name="ffn_down",
    )(act, w_down, x, mod, gn)


def _group(x, mod, per_row, attn_fn, conv_state, s0, chunk, valid_len, pw, n_heads):
    b, l, d = x.shape
    r = b * l
    w = n_heads * LANES
    xf = x.reshape(r, d)
    oattn, ok, ov, ogdn, ozg, osm = _inproj(xf, mod, l, per_row, pw["g_mix_pre"], pw["w_all"], pw["w_gates"],
                                            pw["b_main"], pw["w_small"], pw["b_small"], w)
    a_out = attn_fn(oattn)
    q, k, v, gates = _gdn_prep(ogdn.reshape(b, l, 3 * w), osm.reshape(b, l, LANES), conv_state,
                               pw["conv_w"], pw["a_log"], pw["dt_bias"], n_heads, valid_len)
    g_out, s_new = _gdn(q, k, v, gates, ozg.reshape(b, l, -1), (2 * d) // w, pw["gdn_norm"], s0, chunk, n_heads)
    x1 = _merge(a_out, g_out.reshape(r, w), ozg, xf, mod, l, per_row, pw["w_pa"], pw["w_pg"], pw["w_o"],
                pw["g_mix_post"])
    act = _ffn_up(x1, mod, l, per_row, pw["g_ffn_pre"], pw["w_up"])
    x2 = _ffn_down(act, x1, mod, l, per_row, pw["w_down"], pw["g_ffn_post"])
    kv = (ok.reshape(b, l, n_heads, LANES), ov.reshape(b, l, n_heads, LANES))
    return x2.reshape(b, l, d), kv, ogdn, s_new


def kernel(x_prompt, x_sample, c_prompt, c_sample, cache_k, cache_v, page_table, state_ssm, state_conv, w_ada, b_ada, norm_mix_pre, norm_mix_post, norm_ffn_pre, norm_ffn_post, w_in, b_in, conv_w, lambda_q1, lambda_k1, lambda_q2, lambda_k2, attn_subln, a_log, dt_bias, gdn_norm, w_proj_attn, w_proj_gdn, w_out, w_ffn_up, w_ffn_down):
    bsz, seq, d = x_prompt.shape
    db, s_new, _ = x_sample.shape
    depth = w_in.shape[0]
    n_heads = cache_k.shape[3]
    hd = cache_k.shape[4]
    assert hd == LANES and state_ssm.shape[-1] == LANES and state_ssm.shape[-2] == LANES
    assert state_ssm.shape[2] == n_heads
    w = n_heads * hd
    taps = conv_w.shape[1]
    chunk_p = 64 if seq % 64 == 0 else seq
    slopes = 2.0 ** (-8.0 * jnp.arange(1, n_heads + 1, dtype=F32) / n_heads)
    assert (2 * d) % w == 0
    s_pad = -(-s_new // 8) * 8

    xp, xs = x_prompt, x_sample
    outs = [[] for _ in range(8)]
    for l in range(depth):
        lam_init = 0.8 - 0.6 * math.exp(-0.3 * l)
        lam_vecs = [t[l].reshape(1, -1) for t in (lambda_q1, lambda_k1, lambda_q2, lambda_k2)]
        subg = attn_subln[l].reshape(1, hd)

        c_all = jnp.concatenate([c_prompt, c_sample], axis=0)
        c_all = jnp.pad(c_all, ((0, (-c_all.shape[0]) % 8), (0, 0)))
        mod = _adaln(c_all, w_ada[l], b_ada[l])
        mod_p = mod[:bsz].reshape(bsz, 1, 6 * d)
        mod_s = jnp.repeat(mod[bsz:bsz + db], s_pad, axis=0)

        wt, bi = w_in[l].T, b_in[l]
        n6 = 6 * w
        n7 = 7 * w
        pw = dict(
            w_all=wt, w_gates=wt[n7 + 2 * n_heads:],
            b_main=jnp.concatenate([bi[:n6], bi[n7 + 2 * n_heads:], bi[n6:n7]]).reshape(1, -1),
            w_small=jnp.pad(wt[n7:n7 + 2 * n_heads], ((0, LANES - 2 * n_heads), (0, 0))).astype(BF16),
            b_small=jnp.pad(bi[n7:n7 + 2 * n_heads], (0, LANES - 2 * n_heads)).reshape(1, LANES),
            g_mix_pre=norm_mix_pre[l].reshape(1, d), g_mix_post=norm_mix_post[l].reshape(1, d),
            g_ffn_pre=norm_ffn_pre[l].reshape(1, d), g_ffn_post=norm_ffn_post[l].reshape(1, d),
            conv_w=conv_w[l], a_log=a_log[l], dt_bias=dt_bias[l], gdn_norm=gdn_norm[l].reshape(1, hd),
            w_pa=w_proj_attn[l].astype(BF16), w_pg=w_proj_gdn[l].astype(BF16), w_o=w_out[l].astype(BF16),
            w_up=w_ffn_up[l], w_down=w_ffn_down[l].astype(BF16),
        )

        def attn_p(oattn):
            o = _prompt_attention(oattn.reshape(bsz, seq, 3 * w), slopes, lam_vecs, subg, n_heads, lam_init)
            return o.reshape(bsz * seq, w)

        conv0 = jnp.zeros((bsz, taps - 1, 3 * w), F32)
        s0 = jnp.zeros((bsz, n_heads, hd, hd), F32)
        xp, kv_p, raw_p, ssm_p = _group(xp, mod_p, False, attn_p, conv0, s0, chunk_p, seq, pw, n_heads)

        xs_pad = jnp.pad(xs, ((0, 0), (0, s_pad - s_new), (0, 0)))

        def attn_s(oattn):
            qkv = oattn.reshape(db, s_pad, 3, n_heads, hd)[:, :s_new]
            qh = qkv[:, :, 0].transpose(0, 2, 1, 3)
            qrows = jax.vmap(lambda t: _split_maps(t, (hd // 2) ** -0.5))(qh.reshape(db, n_heads * s_new, hd))
            knew = qkv[:, :, 1].reshape(db, s_new * n_heads, hd)
            vnew = qkv[:, :, 2].reshape(db, s_new * n_heads, hd)
            o = _sample_attention(qrows, knew, vnew, cache_k, cache_v, l, page_table, slopes, lam_vecs, subg,
                                  lam_init)
            o = o.reshape(db, n_heads, s_new, hd).transpose(0, 2, 1, 3).reshape(db, s_new, w)
            return jnp.pad(o, ((0, 0), (0, s_pad - s_new), (0, 0))).reshape(db * s_pad, w).astype(BF16)

        xs_full, kv_s, raw_s, ssm_s = _group(xs_pad, mod_s, True, attn_s, state_conv[l], state_ssm[l], s_pad,
                                             s_new, pw, n_heads)
        xs = xs_full[:, :s_new]

        raw_p = raw_p.reshape(bsz, seq, 3 * w)
        raw_s = jnp.concatenate([state_conv[l], raw_s.reshape(db, s_pad, 3 * w)[:, :s_new]], axis=1)
        for lst, val in zip(outs, (kv_p[0], kv_p[1], kv_s[0][:, :s_new], kv_s[1][:, :s_new], ssm_p, ssm_s,
                                   raw_p[:, seq - (taps - 1):], raw_s[:, s_new:])):
            lst.append(val)

    return (xp, xs) + tuple(jnp.stack(o) for o in outs)
```
